```python
import math
import jax, jax.numpy as jnp
from jax import lax
import numpy as np

D_MODEL = 1024
BATCH = 2
SEQ = 8192
DEPTH = 4
DEC_BATCH = 4
DEC_SEQ = 4096
PAST_LEN = 128

ATT_HEADS = 4
ATT_DK = 64
ATT_DV = 2 * ATT_DK
Q_W = ATT_HEADS * 2 * ATT_DK
K_W = ATT_HEADS * 2 * ATT_DK
ATT_W = ATT_HEADS * ATT_DV
Q_BLOCK = 128
SGU_GROUPS = 4
SGU_CHUNK = 128
SGU_GC = 128
SGU_W = SGU_GROUPS * SGU_GC
S5_GC = 16
S5_GROUPS = 32
S5_N = 64
S5_W = S5_GROUPS * S5_GC
DT_MIN = 0.001
DT_MAX = 0.1
N_BRANCH = 3
IN_SPLITS = (Q_W, Q_W + K_W, Q_W + K_W + ATT_W, Q_W + K_W + ATT_W + 2 * SGU_W,
             Q_W + K_W + ATT_W + 2 * SGU_W + S5_W)
IN_W = IN_SPLITS[-1] + N_BRANCH * D_MODEL
PEER_HEADS = 8
PEER_NKEYS = 128
PEER_EXPERTS = PEER_NKEYS * PEER_NKEYS
PEER_DQ = 256
PEER_DHALF = PEER_DQ // 2
PEER_TOPK = 16
PEER_BLOCK = 128
ALPHA = (2 * DEPTH) ** 0.25
BETA = (8 * DEPTH) ** -0.25
LN_EPS = 1e-5

kernel_name = 'hybrid_diffattn_sgu_s5_peer_encoder'


def layernorm(x, g, b):
    xf = x.astype(jnp.float32)
    mu = jnp.mean(xf, axis=-1, keepdims=True)
    xc = xf - mu
    var = jnp.mean(xc * xc, axis=-1, keepdims=True)
    return (xc * lax.rsqrt(var + LN_EPS) * g.astype(jnp.float32) + b.astype(jnp.float32)).astype(x.dtype)


def rmsnorm(x, g):
    xf = x.astype(jnp.float32)
    return (xf * lax.rsqrt(jnp.mean(xf * xf, axis=-1, keepdims=True) + LN_EPS) * g.astype(jnp.float32)).astype(x.dtype)


def diff_attention(q, k, v, lam):
    B, L = q.shape[0], q.shape[1]
    nblk = L // Q_BLOCK
    slopes = jnp.exp2(-8.0 * jnp.arange(1, ATT_HEADS + 1, dtype=jnp.float32) / ATT_HEADS)
    q = q * (ATT_DK ** -0.5)
    qb = q.reshape(B, nblk, Q_BLOCK, ATT_HEADS, 2, ATT_DK).transpose(1, 0, 2, 3, 4, 5)
    offs = jnp.arange(nblk, dtype=jnp.int32) * Q_BLOCK
    kpos = jnp.arange(L, dtype=jnp.int32)

    def one_block(args):
        qblk, off = args
        s = jnp.einsum('bqhcd,bkhcd->bhcqk', qblk, k).astype(jnp.float32)
        qpos = off + jnp.arange(Q_BLOCK, dtype=jnp.int32)
        dist = jnp.abs(qpos[:, None] - kpos[None, :]).astype(jnp.float32)
        s = s - slopes[None, :, None, None, None] * dist[None, None, None]
        p = jax.nn.softmax(s, axis=-1)
        w = (p[:, :, 0] - lam * p[:, :, 1]).astype(v.dtype)
        return jnp.einsum('bhqk,bkhe->bqhe', w, v)

    o = lax.map(one_block, (qb, offs))
    return o.transpose(1, 0, 2, 3, 4).reshape(B, L, ATT_HEADS, ATT_DV)


def spatial_gating(z, ln_g, ln_b, w_s, b_s):
    B, L = z.shape[0], z.shape[1]
    z = jax.nn.gelu(z)
    u, v = jnp.split(z, 2, axis=-1)
    v = layernorm(v, ln_g, ln_b)
    vb = v.reshape(B, L // SGU_CHUNK, SGU_CHUNK, SGU_GROUPS, SGU_GC)
    sv = jnp.einsum('gts,bnsgc->bntgc', w_s, vb) + b_s.T[None, None, :, :, None]
    return u * sv.reshape(B, L, SGU_W)


def _cplx_combine(c1, c2):
    a1r, a1i, b1r, b1i = c1
    a2r, a2i, b2r, b2i = c2
    return (a2r * a1r - a2i * a1i,
            a2r * a1i + a2i * a1r,
            a2r * b1r - a2i * b1i + b2r,
            a2r * b1i + a2i * b1r + b2i)


def s5_bidirectional(u, a_re, a_im, log_step, b_re, b_im, c_re, c_im, d_skip, glu_w, glu_b):
    B, L = u.shape[0], u.shape[1]
    uf = u.astype(jnp.float32).reshape(B, L, S5_GROUPS, S5_GC)
    y = uf * d_skip.astype(jnp.float32).reshape(S5_GROUPS, S5_GC)
    for dirn in range(2):
        lr = a_re[dirn].astype(jnp.float32)
        li = a_im[dirn].astype(jnp.float32)
        dt = jnp.exp(log_step[dirn].astype(jnp.float32))[:, None]
        mag = jnp.exp(lr * dt)
        abr = mag * jnp.cos(li * dt)
        abi = mag * jnp.sin(li * dt)
        den = lr * lr + li * li
        nr = abr - 1.0
        fr = (nr * lr + abi * li) / den
        fi = (abi * lr - nr * li) / den
        br = b_re[dirn].astype(jnp.float32)
        bi = b_im[dirn].astype(jnp.float32)
        bbr = fr[..., None] * br - fi[..., None] * bi
        bbi = fr[..., None] * bi + fi[..., None] * br
        bur = jnp.einsum('blgc,gnc->blgn', uf, bbr)
        bui = jnp.einsum('blgc,gnc->blgn', uf, bbi)
        afr = jnp.broadcast_to(abr, bur.shape)
        afi = jnp.broadcast_to(abi, bur.shape)
        _, _, xr, xi = lax.associative_scan(_cplx_combine, (afr, afi, bur, bui),
                                            reverse=(dirn == 1), axis=1)
        y = y + jnp.einsum('blgn,gcn->blgc', xr, c_re[dirn].astype(jnp.float32)) \
              - jnp.einsum('blgn,gcn->blgc', xi, c_im[dirn].astype(jnp.float32))
    y = jax.nn.gelu(y.reshape(B, L, S5_W).astype(u.dtype))
    return y * jax.nn.sigmoid(jnp.einsum('ble,ef->blf', y, glu_w) + glu_b)


def peer(x, w_q, keys, u_tab, v_tab):
    B, L, D = x.shape
    xb = x.reshape(-1, PEER_BLOCK, D)

    def one_block(xt):
        P = xt.shape[0]
        q = jnp.einsum('pd,de->pe', xt, w_q).reshape(P, PEER_HEADS, 2, PEER_DHALF)
        s = jnp.einsum('phcd,hckd->phck', q, keys)
        sv, si = lax.top_k(s, PEER_TOPK)
        cand = (sv[:, :, 0, :, None] + sv[:, :, 1, None, :]).reshape(P, PEER_HEADS, PEER_TOPK * PEER_TOPK)
        sc, flat = lax.top_k(cand, PEER_TOPK)
        e1 = jnp.take_along_axis(si[:, :, 0], flat // PEER_TOPK, axis=-1)
        e2 = jnp.take_along_axis(si[:, :, 1], flat % PEER_TOPK, axis=-1)
        e = e1 * PEER_NKEYS + e2
        g = jax.nn.softmax(sc.astype(jnp.float32), axis=-1).astype(xt.dtype)
        hid = jnp.einsum('phkd,pd->phk', u_tab[e], xt)
        return jnp.einsum('phk,phkd->pd', g * jax.nn.gelu(hid), v_tab[e])

    return lax.map(one_block, xb).reshape(B, L, D)


def encoder_layer(x, lam_init, w_in, b_gate, lambda_q1, lambda_k1, lambda_q2, lambda_k2, att_norm_g,
                  sgu_ln_g, sgu_ln_b, sgu_w_s, sgu_b_s,
                  s5_a_re, s5_a_im, s5_log_step, s5_b_re, s5_b_im, s5_c_re, s5_c_im, s5_d, s5_glu_w, s5_glu_b,
                  w_branch, w_o, ln1_g, ln1_b, peer_w_q, peer_keys, peer_u, peer_v, ln2_g, ln2_b):
    B, L, D = x.shape
    h = jnp.einsum('bld,de->ble', x, w_in)
    q, k, v, z_sgu, u_s5, g_logit = jnp.split(h, IN_SPLITS, axis=-1)
    lam = (jnp.exp(jnp.sum(lambda_q1.astype(jnp.float32) * lambda_k1.astype(jnp.float32)))
           - jnp.exp(jnp.sum(lambda_q2.astype(jnp.float32) * lambda_k2.astype(jnp.float32))) + lam_init)
    att = diff_attention(q.reshape(B, L, ATT_HEADS, 2, ATT_DK), k.reshape(B, L, ATT_HEADS, 2, ATT_DK),
                         v.reshape(B, L, ATT_HEADS, ATT_DV), lam)
    att = (rmsnorm(att, att_norm_g) * (1.0 - lam_init)).reshape(B, L, ATT_W)
    sgu = spatial_gating(z_sgu, sgu_ln_g, sgu_ln_b, sgu_w_s, sgu_b_s)
    s5 = s5_bidirectional(u_s5, s5_a_re, s5_a_im, s5_log_step, s5_b_re, s5_b_im, s5_c_re, s5_c_im,
                          s5_d, s5_glu_w, s5_glu_b)
    gates = jax.nn.sigmoid(g_logit.reshape(B, L, N_BRANCH, D) + b_gate)
    branches = jnp.stack([att, sgu, s5], axis=2)
    proj = jnp.einsum('blie,ied->blid', branches, w_branch)
    mixed = jnp.einsum('bld,de->ble', jnp.sum(gates * proj, axis=2), w_o)
    x = layernorm(ALPHA * x + mixed, ln1_g, ln1_b)
    x = layernorm(ALPHA * x + peer(x, peer_w_q, peer_keys, peer_u, peer_v), ln2_g, ln2_b)
    return x


def setup_inputs(seed: int = 0) -> dict:
    key = jax.random.key(seed)
    ks = jax.random.split(key, 40)
    f32 = jnp.float32

    def nrm(i, shape, scale):
        return jax.random.normal(ks[i], shape, f32) * scale

    a_im_base = math.pi * jnp.arange(S5_N, dtype=f32)
    return {
        'x_prompt': nrm(0, (BATCH, SEQ, D_MODEL), 1.0),
        'x_sample': nrm(1, (DEC_BATCH, DEC_SEQ, D_MODEL), 1.0),
        'w_in': nrm(2, (DEPTH, D_MODEL, IN_W), D_MODEL ** -0.5),
        'b_gate': nrm(3, (DEPTH, N_BRANCH, D_MODEL), 0.02),
        'lambda_q1': nrm(4, (DEPTH, ATT_DK), 0.1),
        'lambda_k1': nrm(5, (DEPTH, ATT_DK), 0.1),
        'lambda_q2': nrm(6, (DEPTH, ATT_DK), 0.1),
        'lambda_k2': nrm(7, (DEPTH, ATT_DK), 0.1),
        'att_norm_g': 1.0 + nrm(8, (DEPTH, ATT_DV), 0.02),
        'sgu_ln_g': 1.0 + nrm(9, (DEPTH, SGU_W), 0.02),
        'sgu_ln_b': nrm(10, (DEPTH, SGU_W), 0.02),
        'sgu_w_s': nrm(11, (DEPTH, SGU_GROUPS, SGU_CHUNK, SGU_CHUNK), SGU_CHUNK ** -0.5),
        'sgu_b_s': 1.0 + nrm(12, (DEPTH, SGU_GROUPS, SGU_CHUNK), 0.02),
        's5_a_re': -0.5 + nrm(13, (DEPTH, 2, S5_GROUPS, S5_N), 0.01),
        's5_a_im': a_im_base + nrm(14, (DEPTH, 2, S5_GROUPS, S5_N), 0.01),
        's5_log_step': jax.random.uniform(ks[15], (DEPTH, 2, S5_GROUPS), f32,
                                          minval=math.log(DT_MIN), maxval=math.log(DT_MAX)),
        's5_b_re': nrm(16, (DEPTH, 2, S5_GROUPS, S5_N, S5_GC), (2 * S5_GC) ** -0.5),
        's5_b_im': nrm(17, (DEPTH, 2, S5_GROUPS, S5_N, S5_GC), (2 * S5_GC) ** -0.5),
        's5_c_re': nrm(18, (DEPTH, 2, S5_GROUPS, S5_GC, S5_N), S5_N ** -0.5),
        's5_c_im': nrm(19, (DEPTH, 2, S5_GROUPS, S5_GC, S5_N), S5_N ** -0.5),
        's5_d': nrm(20, (DEPTH, S5_W), 1.0),
        's5_glu_w': nrm(21, (DEPTH, S5_W, S5_W), S5_W ** -0.5),
        's5_glu_b': nrm(22, (DEPTH, S5_W), 0.02),
        'w_branch': nrm(23, (DEPTH, N_BRANCH, ATT_W, D_MODEL), ATT_W ** -0.5),
        'w_o': nrm(24, (DEPTH, D_MODEL, D_MODEL), BETA * D_MODEL ** -0.5),
        'ln1_g': 1.0 + nrm(25, (DEPTH, D_MODEL), 0.02),
        'ln1_b': nrm(26, (DEPTH, D_MODEL), 0.02),
        'peer_w_q': nrm(27, (DEPTH, D_MODEL, PEER_HEADS * PEER_DQ), D_MODEL ** -0.5),
        'peer_keys': nrm(28, (DEPTH, PEER_HEADS, 2, PEER_NKEYS, PEER_DHALF), PEER_DHALF ** -0.5),
        'peer_u': nrm(29, (DEPTH, PEER_EXPERTS, D_MODEL), D_MODEL ** -0.5),
        'peer_v': nrm(30, (DEPTH, PEER_EXPERTS, D_MODEL), BETA * PEER_HEADS ** -0.5),
        'ln2_g': 1.0 + nrm(31, (DEPTH, D_MODEL), 0.02),
        'ln2_b': nrm(32, (DEPTH, D_MODEL), 0.02),
    }


def reference(x_prompt, x_sample, w_in, b_gate, lambda_q1, lambda_k1, lambda_q2, lambda_k2, att_norm_g,
              sgu_ln_g, sgu_ln_b, sgu_w_s, sgu_b_s,
              s5_a_re, s5_a_im, s5_log_step, s5_b_re, s5_b_im, s5_c_re, s5_c_im, s5_d, s5_glu_w, s5_glu_b,
              w_branch, w_o, ln1_g, ln1_b, peer_w_q, peer_keys, peer_u, peer_v, ln2_g, ln2_b):
    def trunk(x):
        for l in range(DEPTH):
            lam_init = 0.8 - 0.6 * math.exp(-0.3 * l)
            x = encoder_layer(x, lam_init, w_in[l], b_gate[l], lambda_q1[l], lambda_k1[l], lambda_q2[l],
                              lambda_k2[l], att_norm_g[l], sgu_ln_g[l], sgu_ln_b[l], sgu_w_s[l], sgu_b_s[l],
                              s5_a_re[l], s5_a_im[l], s5_log_step[l], s5_b_re[l], s5_b_im[l], s5_c_re[l],
                              s5_c_im[l], s5_d[l], s5_glu_w[l], s5_glu_b[l], w_branch[l], w_o[l],
                              ln1_g[l], ln1_b[l], peer_w_q[l], peer_keys[l], peer_u[l], peer_v[l],
                              ln2_g[l], ln2_b[l])
        return x

    y_prompt = trunk(x_prompt)
    y_sample = trunk(x_sample)
    return (y_prompt, y_sample)
```

```python
import functools
import math

import jax
import jax.numpy as jnp
from jax import lax
from jax.experimental import pallas as pl
from jax.experimental.pallas import tpu as pltpu

F32 = jnp.float32
BF16 = jnp.bfloat16

D_MODEL = 1024
DEPTH = 4
ATT_HEADS = 4
ATT_DK = 64
HEAD_W = 2 * ATT_DK
QKV_W = 3 * ATT_HEADS * HEAD_W
SGU_GROUPS = 4
SGU_CHUNK = 128
SGU_W = 512
S5_GC = 16
S5_GROUPS = 32
S5_N = 64
S5_W = 512
S5_STATE = S5_GROUPS * S5_N
S5_SLOTS = 8
S5_BLOCKS = 4
N_BRANCH = 3
Z_OFF = QKV_W
U_OFF = Z_OFF + 2 * SGU_W
G_OFF = U_OFF + S5_W
IN_W = G_OFF + N_BRANCH * D_MODEL
PEER_HEADS = 8
PEER_NKEYS = 128
PEER_EXPERTS = PEER_NKEYS * PEER_NKEYS
PEER_DQ = 256
PEER_DHALF = 128
PEER_TOPK = 16
ALPHA = (2 * DEPTH) ** 0.25
LN_EPS = 1e-5
NEG_BIG = -1e30

VMEM_LIMIT = 56 * 1024 * 1024


def _gelu(x):
    return 0.5 * x * (1.0 + jnp.tanh(0.7978845608028654 * (x + 0.044715 * (x * x * x))))


def _sigmoid(x):
    return 1.0 / (1.0 + jnp.exp(-x))


def _layernorm(x, g, b):
    mu = jnp.mean(x, axis=-1, keepdims=True)
    xc = x - mu
    var = jnp.mean(xc * xc, axis=-1, keepdims=True)
    return xc * lax.rsqrt(var + LN_EPS) * g + b


def _const_spec(shape):
    nd = len(shape)
    return pl.BlockSpec(shape, lambda *_: (0,) * nd, pipeline_mode=pl.Buffered(1))


def _params(sem):
    return pltpu.CompilerParams(dimension_semantics=sem, vmem_limit_bytes=VMEM_LIMIT)


def _inproj_kernel(x_ref, w_ref, bg_ref, lng_ref, lnb_ref, ws_ref, bs_ref,
                   qkv_ref, sgu_ref, us5_ref, gate_ref, *, tm):
    xb = x_ref[...].astype(BF16)

    h = jnp.dot(xb, w_ref[:, 0:QKV_W], preferred_element_type=F32)
    nq = ATT_HEADS * HEAD_W
    qkv_ref[:, 0:nq] = (h[:, 0:nq] * (ATT_DK ** -0.5)).astype(BF16)
    qkv_ref[:, nq:QKV_W] = h[:, nq:QKV_W].astype(BF16)

    z = _gelu(jnp.dot(xb, w_ref[:, Z_OFF:U_OFF], preferred_element_type=F32))
    u = z[:, 0:SGU_W]
    v = _layernorm(z[:, SGU_W:2 * SGU_W], lng_ref[...], lnb_ref[...]).astype(BF16)
    for ci in range(tm // SGU_CHUNK):
        r0 = ci * SGU_CHUNK
        for g in range(SGU_GROUPS):
            c0 = g * 128
            sv = jnp.dot(ws_ref[g], v[r0:r0 + SGU_CHUNK, c0:c0 + 128],
                         preferred_element_type=F32) + bs_ref[:, c0:c0 + 128]
            sgu_ref[r0:r0 + SGU_CHUNK, c0:c0 + 128] = (
                u[r0:r0 + SGU_CHUNK, c0:c0 + 128] * sv).astype(BF16)

    us5_ref[...] = jnp.dot(xb, w_ref[:, U_OFF:G_OFF], preferred_element_type=F32)

    gl = jnp.dot(xb, w_ref[:, G_OFF:IN_W], preferred_element_type=F32)
    gate_ref[...] = _sigmoid(gl + bg_ref[...])


def _in_proj(x, w_in, b_gate, ln_g, ln_b, w_s, b_s_full, *, tm=256):
    n = x.shape[0]
    row = lambda w: pl.BlockSpec((tm, w), lambda i: (i, 0))
    return pl.pallas_call(
        functools.partial(_inproj_kernel, tm=tm),
        grid=(n // tm,),
        in_specs=[row(D_MODEL), _const_spec((D_MODEL, IN_W)), _const_spec((1, N_BRANCH * D_MODEL)),
                  _const_spec((1, SGU_W)), _const_spec((1, SGU_W)),
                  _const_spec((SGU_GROUPS, SGU_CHUNK, SGU_CHUNK)), _const_spec((SGU_CHUNK, SGU_W))],
        out_specs=[row(QKV_W), row(SGU_W), row(S5_W), row(N_BRANCH * D_MODEL)],
        out_shape=[jax.ShapeDtypeStruct((n, QKV_W), BF16), jax.ShapeDtypeStruct((n, SGU_W), BF16),
                   jax.ShapeDtypeStruct((n, S5_W), F32),
                   jax.ShapeDtypeStruct((n, N_BRANCH * D_MODEL), F32)],
        compiler_params=_params(("parallel",)),
        name="in_proj",
    )(x, w_in, b_gate, ln_g, ln_b, w_s, b_s_full)


def _attn_kernel(sc_ref, lamp_ref, q_ref, k_ref, v_ref, g_ref, o_ref,
                 qs_ref, rel_ref, m_ref, l_ref, acc_ref, *, tq, tk, seq):
    hd = pl.program_id(1)
    qi = pl.program_id(2)
    slope = sc_ref[hd]
    nk = seq // tk
    jd = (qi * tq) // tk
    rows = 2 * tq

    q = q_ref[...]
    lane = lax.broadcasted_iota(jnp.int32, q.shape, 1)
    zero = jnp.zeros_like(q)
    qs_ref[0:tq, :] = jnp.where(lane < ATT_DK, q, zero)
    qs_ref[tq:rows, :] = jnp.where(lane >= ATT_DK, q, zero)

    ri = lax.broadcasted_iota(jnp.int32, (rows, tk), 0)
    ri = jnp.where(ri >= tq, ri - tq, ri)
    ci = lax.broadcasted_iota(jnp.int32, (rows, tk), 1)
    rel_ref[...] = (ri - ci).astype(F32)

    m_ref[...] = jnp.full(m_ref.shape, NEG_BIG, F32)
    l_ref[...] = jnp.zeros(l_ref.shape, F32)
    acc_ref[...] = jnp.zeros(acc_ref.shape, F32)

    def step(j, bias_fn):
        k0 = pl.multiple_of(j * tk, tk)
        kb = k_ref[pl.ds(k0, tk), :]
        vb = v_ref[pl.ds(k0, tk), :]
        s = lax.dot_general(qs_ref[...], kb, (((1,), (1,)), ((), ())),
                            preferred_element_type=F32)
        off = (qi * tq - j * tk).astype(F32)
        s, c = bias_fn(s, off)
        m_old = m_ref[...]
        m_new = jnp.maximum(m_old, jnp.max(s, axis=1, keepdims=True) + c)
        alpha = jnp.exp(m_old - m_new)
        p = jnp.exp(s - (m_new - c))
        l_ref[...] = alpha * l_ref[...] + jnp.sum(p, axis=1, keepdims=True)
        acc_ref[...] = alpha * acc_ref[...] + jnp.dot(p.astype(BF16), vb,
                                                      preferred_element_type=F32)
        m_ref[...] = m_new

    def left(s, off):
        return s - slope * rel_ref[...], -slope * off

    def right(s, off):
        return s + slope * rel_ref[...], slope * off

    def diag(s, off):
        return s - slope * jnp.abs(rel_ref[...] + off), 0.0

    lax.fori_loop(0, jd, lambda j, _: (step(j, left), 0)[1], 0)
    step(jd, diag)
    lax.fori_loop(jd + 1, nk, lambda j, _: (step(j, right), 0)[1], 0)

    lp = lamp_ref[...]
    lam = (jnp.exp(jnp.sum(lp[0:1] * lp[1:2], axis=1, keepdims=True))
           - jnp.exp(jnp.sum(lp[2:3] * lp[3:4], axis=1, keepdims=True)) + sc_ref[ATT_HEADS])
    acc = acc_ref[...]
    l = l_ref[...]
    o = acc[0:tq] / l[0:tq] - lam * (acc[tq:rows] / l[tq:rows])
    ms =jnp.mean(o * o, axis=-1, keepdims=True)
    o_ref[...] = (o * lax.rsqrt(ms + LN_EPS) * g_ref[...] * sc_ref[ATT_HEADS + 1]).astype(o_ref.dtype)


def _attention(scal, lam_params, qkv, norm_g, *, row_off, batch, seq, n_rows, tq=256, tk=1024):
    tk = min(tk, seq)
    qb0 = row_off // tq
    kb0 = row_off // seq
    nqb = seq // tq
    return pl.pallas_call(
        functools.partial(_attn_kernel, tq=tq, tk=tk, seq=seq),
        grid=(batch, ATT_HEADS, nqb),
        in_specs=[
            pl.BlockSpec(memory_space=pltpu.SMEM),
            pl.BlockSpec((4, ATT_DK), lambda b, h, i: (0, 0)),
            pl.BlockSpec((tq, HEAD_W), lambda b, h, i: (qb0 + b * nqb + i, h)),
            pl.BlockSpec((seq, HEAD_W), lambda b, h, i: (kb0 + b, ATT_HEADS + h)),
            pl.BlockSpec((seq, HEAD_W), lambda b, h, i: (kb0 + b, 2 * ATT_HEADS + h)),
            pl.BlockSpec((1, HEAD_W), lambda b, h, i: (0, 0)),
        ],
        out_specs=pl.BlockSpec((tq, HEAD_W), lambda b, h, i: (b * nqb + i, h)),
        out_shape=jax.ShapeDtypeStruct((n_rows, ATT_HEADS * HEAD_W), BF16),
        scratch_shapes=[pltpu.VMEM((2 * tq, HEAD_W), BF16), pltpu.VMEM((2 * tq, tk), F32),
                        pltpu.VMEM((2 * tq, 1), F32), pltpu.VMEM((2 * tq, 1), F32),
                        pltpu.VMEM((2 * tq, HEAD_W), F32)],
        compiler_params=_params(("parallel", "parallel", "parallel")),
        name="diff_attention",
    )(scal, lam_params, qkv, qkv, qkv, norm_g)


def _s5_kernel(u_ref, wb_ref, wcr_ref, wci_ref, ar_ref, ai_ref, y_ref,
               br_ref, bi_ref, xr_ref, xi_ref, *, tc):
    half = S5_STATE // S5_BLOCKS

    @pl.when(pl.program_id(0) == 0)
    def _():
        xr_ref[...] = jnp.zeros(xr_ref.shape, F32)
        xi_ref[...] = jnp.zeros(xi_ref.shape, F32)

    for j in range(S5_BLOCKS):
        bu = jnp.dot(u_ref[:, 256 * j:256 * (j + 1)], wb_ref[j], preferred_element_type=F32)
        br_ref[:, half * j:half * (j + 1)] = bu[:, 0:half]
        bi_ref[:, half * j:half * (j + 1)] = bu[:, half:2 * half]

    def body(t, carry):
        xr, xi = carry
        r0 = pl.multiple_of(t * S5_SLOTS, S5_SLOTS)
        ar = ar_ref[...]
        ai = ai_ref[...]
        nxr = ar * xr - ai * xi + br_ref[pl.ds(r0, S5_SLOTS), :]
        nxi = ar * xi + ai * xr + bi_ref[pl.ds(r0, S5_SLOTS), :]
        br_ref[pl.ds(r0, S5_SLOTS), :] = nxr
        bi_ref[pl.ds(r0, S5_SLOTS), :] = nxi
        return nxr, nxi

    xr, xi = lax.fori_loop(0, tc, body, (xr_ref[...], xi_ref[...]), unroll=4)
    xr_ref[...] = xr
    xi_ref[...] = xi

    for j in range(S5_BLOCKS):
        sr = br_ref[:, half * j:half * (j + 1)].astype(BF16)
        si = bi_ref[:, half * j:half * (j + 1)].astype(BF16)
        y_ref[:, 256 * j:256 * (j + 1)] = (
            jnp.dot(sr, wcr_ref[j], preferred_element_type=F32)
            + jnp.dot(si, wci_ref[j], preferred_element_type=F32))


def _s5_scan(u8, wb, wcr, wci, ar8, ai8, *, seq, tc=128):
    rows = tc * S5_SLOTS
    return pl.pallas_call(
        functools.partial(_s5_kernel, tc=tc),
        grid=(seq // tc,),
        in_specs=[pl.BlockSpec((rows, 2 * S5_W), lambda n: (n, 0)),
                  _const_spec((S5_BLOCKS, 256, 1024)), _const_spec((S5_BLOCKS, 512, 256)),
                  _const_spec((S5_BLOCKS, 512, 256)),
                  _const_spec((S5_SLOTS, S5_STATE)), _const_spec((S5_SLOTS, S5_STATE))],
        out_specs=pl.BlockSpec((rows, 2 * S5_W), lambda n: (n, 0)),
        out_shape=jax.ShapeDtypeStruct((seq * S5_SLOTS, 2 * S5_W), F32),
        scratch_shapes=[pltpu.VMEM((rows, S5_STATE), F32), pltpu.VMEM((rows, S5_STATE), F32),
                        pltpu.VMEM((S5_SLOTS, S5_STATE), F32), pltpu.VMEM((S5_SLOTS, S5_STATE), F32)],
        compiler_params=_params(("arbitrary",)),
        name="s5_scan",
    )(u8, wb, wcr, wci, ar8, ai8)


def _s5_weights(a_re, a_im, log_step, b_re, b_im, c_re, c_im):
    dt = jnp.exp(log_step)[..., None]
    mag = jnp.exp(a_re * dt)
    abr = mag * jnp.cos(a_im * dt)
    abi = mag * jnp.sin(a_im * dt)
    den = a_re * a_re + a_im * a_im
    nr = abr - 1.0
    fr = (nr * a_re + abi * a_im) / den
    fi = (abi * a_re - nr * a_im) / den
    bbr = fr[..., None] * b_re - fi[..., None] * b_im
    bbi = fr[..., None] * b_im + fi[..., None] * b_re
    eye = jnp.eye(8, dtype=F32)

    def in_block(bb):
        bb = bb.reshape(2, S5_BLOCKS, 8, S5_N, S5_GC)
        return jnp.einsum('djgnc,gh->jdgchn', bb, eye).reshape(S5_BLOCKS, 256, 512)

    wb = jnp.concatenate([in_block(bbr), in_block(bbi)], axis=-1).astype(BF16)

    def out_block(cc):
        cc = cc.reshape(2, S5_BLOCKS, 8, S5_GC, S5_N)
        return jnp.einsum('djgcn,gh->jgndhc', cc, eye).reshape(S5_BLOCKS, 512, 256)

    wcr = out_block(c_re).astype(BF16)
    wci = out_block(-c_im).astype(BF16)
    return wb, wcr, wci, abr.reshape(2, S5_STATE), abi.reshape(2, S5_STATE)


def _s5_branch(us5, wb, wcr, wci, abr, abi, *, batch, seq):
    assert 2 * batch <= S5_SLOTS
    u = us5.astype(BF16).reshape(batch, seq, S5_BLOCKS, 128)
    zero = jnp.zeros_like(u)
    fwd = jnp.concatenate([u, zero], axis=-1)
    bwd = jnp.concatenate([zero, u[:, ::-1]], axis=-1)
    pad = jnp.zeros((S5_SLOTS - 2 * batch,) + fwd.shape[1:], BF16)
    u8 = jnp.concatenate([fwd, bwd, pad], axis=0)
    u8 = u8.transpose(1, 0, 2, 3).reshape(seq * S5_SLOTS, 2 * S5_W)
    slot_dir = jnp.array([0] * batch + [1] * batch, jnp.int32)
    live = jnp.concatenate([jnp.ones((2 * batch, 1), F32),
                            jnp.zeros((S5_SLOTS - 2 * batch, 1), F32)], axis=0)
    sel = jnp.concatenate([slot_dir, jnp.zeros((S5_SLOTS - 2 * batch,), jnp.int32)])
    ar8 = abr[sel] * live
    ai8 = abi[sel] * live
    y8 = _s5_scan(u8, wb, wcr, wci, ar8, ai8, seq=seq)
    y8 = y8.reshape(seq, S5_SLOTS, S5_BLOCKS, 2, 128)
    yf = y8[:, 0:batch, :, 0, :].transpose(1, 0, 2, 3).reshape(batch * seq, S5_W)
    yb = y8[::-1, batch:2 * batch, :, 1, :].transpose(1, 0, 2, 3).reshape(batch * seq, S5_W)
    return yf, yb


def _merge_kernel(x_ref, att_ref, sgu_ref, us5_ref, yf_ref, yb_ref, gate_ref,
                  d_ref, gw_ref, gb_ref, wbr_ref, wo_ref, lg_ref, lb_ref, o_ref):
    y = _gelu(us5_ref[...] * d_ref[...] + yf_ref[...] + yb_ref[...])
    glu = _sigmoid(jnp.dot(y.astype(BF16), gw_ref[...], preferred_element_type=F32) + gb_ref[...])
    s5 = (y * glu).astype(BF16)
    branches = (att_ref[...], sgu_ref[...], s5)
    mix = None
    for i in range(N_BRANCH):
        proj = jnp.dot(branches[i], wbr_ref[i], preferred_element_type=F32)
        term = gate_ref[:, i * D_MODEL:(i + 1) * D_MODEL] * proj
        mix = term if mix is None else mix + term
    mixed = jnp.dot(mix.astype(BF16), wo_ref[...], preferred_element_type=F32)
    o_ref[...] = _layernorm(ALPHA * x_ref[...] + mixed, lg_ref[...], lb_ref[...])


def _merge(x, att, sgu, us5, yf, yb, gates, s5_d, glu_w, glu_b, w_branch, w_o, ln_g, ln_b, *, tm=512):
    n = x.shape[0]
    row = lambda w: pl.BlockSpec((tm, w), lambda i: (i, 0))
    return pl.pallas_call(
        _merge_kernel,
        grid=(n // tm,),
        in_specs=[row(D_MODEL), row(S5_W), row(S5_W), row(S5_W), row(S5_W), row(S5_W),
                  row(N_BRANCH * D_MODEL),
                  _const_spec((1, S5_W)), _const_spec((S5_W, S5_W)), _const_spec((1, S5_W)),
                  _const_spec((N_BRANCH, S5_W, D_MODEL)), _const_spec((D_MODEL, D_MODEL)),
                  _const_spec((1, D_MODEL)), _const_spec((1, D_MODEL))],
        out_specs=row(D_MODEL),
        out_shape=jax.ShapeDtypeStruct((n, D_MODEL), F32),
        compiler_params=_params(("parallel",)),
        name="merge",
    )(x, att, sgu, us5, yf, yb, gates, s5_d, glu_w, glu_b, w_branch, w_o, ln_g, ln_b)


def _top_rows(s, k):
    row = lax.broadcasted_iota(jnp.int32, (k, s.shape[1]), 0)
    out = jnp.zeros((k, s.shape[1]), F32)
    for i in range(k):
        m = jnp.max(s, axis=0, keepdims=True)
        out = jnp.where(row == i, m, out)
        s = jnp.where(s == m, -jnp.inf, s)
    return out


def _peer_kernel(x_ref, wq_ref, keys_ref, u_ref, vt_ref, lg_ref, lb_ref, o_ref,
                 xb_ref, thr_ref, e0_ref, s1_ref, e1_ref, acc_ref, *, tm, rpc):
    c = pl.program_id(1)

    @pl.when(c == 0)
    def _():
        xb = x_ref[...].astype(BF16)
        xb_ref[...] = xb
        q = jnp.dot(xb, wq_ref[...], preferred_element_type=F32).astype(BF16)
        for h in range(PEER_HEADS):
            st = []
            for half in range(2):
                c0 = (2 * h + half) * PEER_DHALF
                st.append(lax.dot_general(keys_ref[h, half], q[:, c0:c0 + PEER_DHALF],
                                          (((1,), (1,)), ((), ())), preferred_element_type=F32))
            v0 = _top_rows(st[0], PEER_TOPK)
            v1 = _top_rows(st[1], PEER_TOPK)
            cand = jnp.concatenate([v0[i:i + 1] + v1 for i in range(PEER_TOPK)], axis=0)
            cs = _top_rows(cand, PEER_TOPK)
            tau = cs[PEER_TOPK - 1:PEER_TOPK]
            z = jnp.sum(jnp.exp(cs - cs[0:1]), axis=0, keepdims=True)
            thr_ref[h] = tau - st[0]
            e0_ref[h] = jnp.exp(st[0] - v0[0:1]) / z
            s1_ref[h] = st[1]
            e1_ref[h] = jnp.exp(st[1] - v1[0:1])
        acc_ref[...] = jnp.zeros(acc_ref.shape, F32)

    hid = lax.dot_general(u_ref[...], xb_ref[...], (((1,), (1,)), ((), ())),
                          preferred_element_type=F32)
    parts = []
    for r in range(rpc):
        e1 = c * rpc + r
        w = jnp.zeros((PEER_NKEYS, tm), F32)
        for h in range(PEER_HEADS):
            t = thr_ref[h, pl.ds(e1, 1), :]
            w = w + jnp.where(s1_ref[h] >= t, e1_ref[h], 0.0) * e0_ref[h, pl.ds(e1, 1), :]
        parts.append(w)
    gate = jnp.concatenate(parts, axis=0)
    act = (gate * _gelu(hid)).astype(BF16)
    acc_ref[...] += jnp.dot(vt_ref[...], act, preferred_element_type=F32)

    @pl.when(c == pl.num_programs(1) - 1)
    def _():
        out = acc_ref[...].T
        o_ref[...] = _layernorm(ALPHA * x_ref[...] + out, lg_ref[...], lb_ref[...])


def _peer(x, w_q, keys, u_tab, v_tab_t, ln_g, ln_b, *, tm=512, rpc=4):
    n = x.shape[0]
    ec = rpc * PEER_NKEYS
    return pl.pallas_call(
        functools.partial(_peer_kernel, tm=tm, rpc=rpc),
        grid=(n // tm, PEER_EXPERTS // ec),
        in_specs=[pl.BlockSpec((tm, D_MODEL), lambda i, c: (i, 0)),
                  _const_spec((D_MODEL, PEER_HEADS * PEER_DQ)),
                  _const_spec((PEER_HEADS, 2, PEER_NKEYS, PEER_DHALF)),
                  pl.BlockSpec((ec, D_MODEL), lambda i, c: (c, 0)),
                  pl.BlockSpec((D_MODEL, ec), lambda i, c: (0, c)),
                  _const_spec((1, D_MODEL)), _const_spec((1, D_MODEL))],
        out_specs=pl.BlockSpec((tm, D_MODEL), lambda i, c: (i, 0)),
        out_shape=jax.ShapeDtypeStruct((n, D_MODEL), F32),
        scratch_shapes=[pltpu.VMEM((tm, D_MODEL), BF16),
                        pltpu.VMEM((PEER_HEADS, PEER_NKEYS, tm), F32),
                        pltpu.VMEM((PEER_HEADS, PEER_NKEYS, tm), F32),
                        pltpu.VMEM((PEER_HEADS, PEER_NKEYS, tm), F32),
                        pltpu.VMEM((PEER_HEADS, PEER_NKEYS, tm), F32),
                        pltpu.VMEM((D_MODEL, tm), F32)],
        compiler_params=_params(("parallel", "arbitrary")),
        name="peer",
    )(x, w_q, keys, u_tab, v_tab_t, ln_g, ln_b)


def _layer(x, p, trunks):
    n = x.shape[0]
    qkv, sgu, us5, gates = _in_proj(x, p['w_in'], p['b_gate'], p['sgu_ln_g'], p['sgu_ln_b'],
                                    p['sgu_w_s'], p['sgu_b_s'])
    wb, wcr, wci, abr, abi = _s5_weights(p['s5_a_re'], p['s5_a_im'], p['s5_log_step'],
                                         p['s5_b_re'], p['s5_b_im'], p['s5_c_re'], p['s5_c_im'])
    att, yf, yb = [], [], []
    for row_off, batch, seq in trunks:
        nr = batch * seq
        att.append(_attention(p['att_scal'], p['lam_params'], qkv, p['att_norm_g'],
                              row_off=row_off, batch=batch, seq=seq, n_rows=nr))
        f, b = _s5_branch(lax.slice_in_dim(us5, row_off, row_off + nr, axis=0),
                          wb, wcr, wci, abr, abi, batch=batch, seq=seq)
        yf.append(f)
        yb.append(b)
    att = jnp.concatenate(att, axis=0)
    yf = jnp.concatenate(yf, axis=0)
    yb = jnp.concatenate(yb, axis=0)
    x = _merge(x, att, sgu, us5, yf, yb, gates, p['s5_d'], p['s5_glu_w'], p['s5_glu_b'],
               p['w_branch'], p['w_o'], p['ln1_g'], p['ln1_b'])
    x = _peer(x, p['peer_w_q'], p['peer_keys'], p['peer_u'], p['peer_v_t'], p['ln2_g'], p['ln2_b'])
    return x


def _prepare(w_in, b_gate, lambda_q1, lambda_k1, lambda_q2, lambda_k2, att_norm_g,
             sgu_ln_g, sgu_ln_b, sgu_w_s, sgu_b_s,
             s5_a_re, s5_a_im, s5_log_step, s5_b_re, s5_b_im, s5_c_re, s5_c_im, s5_d, s5_glu_w, s5_glu_b,
             w_branch, w_o, ln1_g, ln1_b, peer_w_q, peer_keys, peer_u, peer_v, ln2_g, ln2_b):
    depth = w_in.shape[0]
    lam_init = 0.8 - 0.6 * jnp.exp(-0.3 * jnp.arange(depth, dtype=F32))
    slopes = jnp.exp2(-8.0 * jnp.arange(1, ATT_HEADS + 1, dtype=F32) / ATT_HEADS)
    att_scal = jnp.concatenate([jnp.broadcast_to(slopes, (depth, ATT_HEADS)),
                                lam_init[:, None], 1.0 - lam_init[:, None]], axis=1)
    row = lambda a: a.reshape(depth, 1, -1).astype(F32)
    return dict(
        w_in=w_in.astype(BF16), b_gate=row(b_gate),
        att_scal=att_scal,
        lam_params=jnp.stack([lambda_q1, lambda_k1, lambda_q2, lambda_k2], axis=1).astype(F32),
        att_norm_g=row(att_norm_g),
        sgu_ln_g=row(sgu_ln_g), sgu_ln_b=row(sgu_ln_b), sgu_w_s=sgu_w_s.astype(BF16),
        sgu_b_s=jnp.repeat(jnp.swapaxes(sgu_b_s, 1, 2), SGU_W // SGU_GROUPS, axis=2).astype(F32),
        s5_a_re=s5_a_re, s5_a_im=s5_a_im, s5_log_step=s5_log_step, s5_b_re=s5_b_re, s5_b_im=s5_b_im,
        s5_c_re=s5_c_re, s5_c_im=s5_c_im,
        s5_d=row(s5_d), s5_glu_w=s5_glu_w.astype(BF16), s5_glu_b=row(s5_glu_b),
        w_branch=w_branch.astype(BF16), w_o=w_o.astype(BF16), ln1_g=row(ln1_g), ln1_b=row(ln1_b),
        peer_w_q=peer_w_q.astype(BF16), peer_keys=peer_keys.astype(BF16),
        peer_u=peer_u.astype(BF16), peer_v_t=jnp.swapaxes(peer_v.astype(BF16), 1, 2),
        ln2_g=row(ln2_g), ln2_b=row(ln2_b),
    )


def _trunk(xs, weights):
    trunks, off = [], 0
    for x in xs:
        b, s, _ = x.shape
        trunks.append((off, b, s))
        off += b * s
    x = jnp.concatenate([x.reshape(-1, D_MODEL) for x in xs], axis=0)
    params = _prepare(*weights)
    trunks = tuple(trunks)
    x, _ = lax.scan(lambda carry, p: (_layer(carry, p, trunks), None), x, params)
    outs = []
    for (off, b, s), xi in zip(trunks, xs):
        outs.append(x[off:off + b * s].reshape(xi.shape))
    return tuple(outs)


def kernel(x_prompt, x_sample, w_in, b_gate, lambda_q1, lambda_k1, lambda_q2, lambda_k2, att_norm_g, sgu_ln_g, sgu_ln_b, sgu_w_s, sgu_b_s, s5_a_re, s5_a_im, s5_log_step, s5_b_re, s5_b_im, s5_c_re, s5_c_im, s5_d, s5_glu_w, s5_glu_b, w_branch, w_o, ln1_g, ln1_b, peer_w_q, peer_keys, peer_u, peer_v, ln2_g, ln2_b):
    weights = (w_in, b_gate, lambda_q1, lambda_k1, lambda_q2, lambda_k2, att_norm_g,
               sgu_ln_g, sgu_ln_b, sgu_w_s, sgu_b_s,
               s5_a_re, s5_a_im, s5_log_step, s5_b_re, s5_b_im, s5_c_re, s5_c_im, s5_d, s5_glu_w, s5_glu_b,
               w_branch, w_o, ln1_g, ln1_b, peer_w_q, peer_keys, peer_u, peer_v, ln2_g, ln2_b)
    y_prompt, y_sample = _trunk([x_prompt, x_sample], weights)
    return (y_prompt, y_sample)
```

```python
import functools
import math

import jax
import jax.numpy as jnp
from jax import lax
from jax.experimental import pallas as pl
from jax.experimental.pallas import tpu as pltpu

F32 = jnp.float32
BF16 = jnp.bfloat16

D_MODEL = 1024
DEPTH = 4
ATT_HEADS = 4
ATT_DK = 64
HEAD_W = 2 * ATT_DK
QKV_W = 3 * ATT_HEADS * HEAD_W
SGU_GROUPS = 4
SGU_CHUNK = 128
SGU_W = 512
S5_GC = 16
S5_GROUPS = 32
S5_N = 64
S5_W = 512
S5_STATE = S5_GROUPS * S5_N
S5_SLOTS = 8
S5_BLOCKS = 4
N_BRANCH = 3
Z_OFF = QKV_W
U_OFF = Z_OFF + 2 * SGU_W
G_OFF = U_OFF + S5_W
IN_W = G_OFF + N_BRANCH * D_MODEL
PEER_HEADS = 8
PEER_NKEYS = 128
PEER_EXPERTS = PEER_NKEYS * PEER_NKEYS
PEER_DQ = 256
PEER_DHALF = 128
PEER_TOPK = 16
ALPHA = (2 * DEPTH) ** 0.25
LN_EPS = 1e-5
NEG_BIG = -1e30

VMEM_LIMIT = 56 * 1024 * 1024


def _gelu(x):
    return 0.5 * x * (1.0 + jnp.tanh(0.7978845608028654 * (x + 0.044715 * (x * x * x))))


def _sigmoid(x):
    return 1.0 / (1.0 + jnp.exp(-x))


def _layernorm(x, g, b):
    mu = jnp.mean(x, axis=-1, keepdims=True)
    xc = x - mu
    var = jnp.mean(xc * xc, axis=-1, keepdims=True)
    return xc * lax.rsqrt(var + LN_EPS) * g + b


def _const_spec(shape):
    nd = len(shape)
    return pl.BlockSpec(shape, lambda *_: (0,) * nd, pipeline_mode=pl.Buffered(1))


def _params(sem, flags=None):
    return pltpu.CompilerParams(dimension_semantics=sem, vmem_limit_bytes=VMEM_LIMIT, flags=flags)


def _inproj_kernel(x_ref, w_ref, bg_ref, lng_ref, lnb_ref, ws_ref, bs_ref,
                   qkv_ref, sgu_ref, us5_ref, gate_ref, *, tm):
    xb = x_ref[...].astype(BF16)

    h = jnp.dot(xb, w_ref[:, 0:QKV_W], preferred_element_type=F32)
    nq = ATT_HEADS * HEAD_W
    qkv_ref[:, 0:nq] = (h[:, 0:nq] * (ATT_DK ** -0.5)).astype(BF16)
    qkv_ref[:, nq:QKV_W] = h[:, nq:QKV_W].astype(BF16)

    z = _gelu(jnp.dot(xb, w_ref[:, Z_OFF:U_OFF], preferred_element_type=F32))
    u = z[:, 0:SGU_W]
    v = _layernorm(z[:, SGU_W:2 * SGU_W], lng_ref[...], lnb_ref[...]).astype(BF16)
    for ci in range(tm // SGU_CHUNK):
        r0 = ci * SGU_CHUNK
        for g in range(SGU_GROUPS):
            c0 = g * 128
            sv = jnp.dot(ws_ref[g], v[r0:r0 + SGU_CHUNK, c0:c0 + 128],
                         preferred_element_type=F32) + bs_ref[:, c0:c0 + 128]
            sgu_ref[r0:r0 + SGU_CHUNK, c0:c0 + 128] = (
                u[r0:r0 + SGU_CHUNK, c0:c0 + 128] * sv).astype(BF16)

    us5_ref[...] = jnp.dot(xb, w_ref[:, U_OFF:G_OFF], preferred_element_type=F32)

    gl = jnp.dot(xb, w_ref[:, G_OFF:IN_W], preferred_element_type=F32)
    gate_ref[...] = _sigmoid(gl + bg_ref[...])


def _in_proj(x, w_in, b_gate, ln_g, ln_b, w_s, b_s_full, *, tm=256):
    n = x.shape[0]
    row = lambda w: pl.BlockSpec((tm, w), lambda i: (i, 0))
    return pl.pallas_call(
        functools.partial(_inproj_kernel, tm=tm),
        grid=(n // tm,),
        in_specs=[row(D_MODEL), _const_spec((D_MODEL, IN_W)), _const_spec((1, N_BRANCH * D_MODEL)),
                  _const_spec((1, SGU_W)), _const_spec((1, SGU_W)),
                  _const_spec((SGU_GROUPS, SGU_CHUNK, SGU_CHUNK)), _const_spec((SGU_CHUNK, SGU_W))],
        out_specs=[row(QKV_W), row(SGU_W), row(S5_W), row(N_BRANCH * D_MODEL)],
        out_shape=[jax.ShapeDtypeStruct((n, QKV_W), BF16), jax.ShapeDtypeStruct((n, SGU_W), BF16),
                   jax.ShapeDtypeStruct((n, S5_W), F32),
                   jax.ShapeDtypeStruct((n, N_BRANCH * D_MODEL), F32)],
        compiler_params=_params(("parallel",)),
        name="in_proj",
    )(x, w_in, b_gate, ln_g, ln_b, w_s, b_s_full)


def _attn_kernel(sc_ref, lamp_ref, q_ref, k_ref, v_ref, g_ref, o_ref,
                 qs_ref, bias_ref, sa_ref, sb_ref, mxa_ref, mxb_ref, m_ref, l_ref, acc_ref,
                 *, tq, tk, seq):
    hd = pl.program_id(1)
    qi = pl.program_id(2)
    slope = sc_ref[hd]
    nk = seq // tk
    nd = tk // tq
    jd = (qi * tq) // tk
    rows = 2 * tq
    assert nk == 1 or nk % 2 == 0

    @pl.when(qi == 0)
    def _():
        ri = lax.broadcasted_iota(jnp.int32, (rows, tk), 0)
        ri = jnp.where(ri >= tq, ri - tq, ri)
        ci = lax.broadcasted_iota(jnp.int32, (rows, tk), 1)
        rel = (ri - ci).astype(F32)
        bias_ref[0] = slope * rel
        bias_ref[1] = -slope * rel
        for v in range(nd):
            bias_ref[2 + v] = slope * jnp.abs(rel + float(v * tq))

    q = q_ref[...]
    lane = lax.broadcasted_iota(jnp.int32, q.shape, 1)
    zero = jnp.zeros_like(q)
    qs_ref[0:tq, :] = jnp.where(lane < ATT_DK, q, zero)
    qs_ref[tq:rows, :] = jnp.where(lane >= ATT_DK, q, zero)

    m_ref[...] = jnp.full(m_ref.shape, NEG_BIG, F32)
    l_ref[...] = jnp.zeros(l_ref.shape, F32)
    acc_ref[...] = jnp.zeros(acc_ref.shape, F32)

    def tile_const(j):
        off = jnp.abs(qi * tq - j * tk).astype(F32)
        return jnp.where(j == jd, 0.0, -slope * off)

    def key_start(j):
        return j * tk if isinstance(j, int) else pl.multiple_of(j * tk, tk)

    def scores(j, s_ref, mx_ref):
        k0 = key_start(j)
        s = lax.dot_general(qs_ref[...], k_ref[pl.ds(k0, tk), :], (((1,), (1,)), ((), ())),
                            preferred_element_type=F32)
        typ = jnp.where(j < jd, 0, jnp.where(j > jd, 1, 2 + qi % nd))
        s = s - bias_ref[typ]
        s_ref[...] = s
        mx_ref[...] = jnp.max(s, axis=1, keepdims=True) + tile_const(j)

    def accumulate(j, s_ref, mx_ref):
        k0 = key_start(j)
        m_old = m_ref[...]
        m_new = jnp.maximum(m_old, mx_ref[...])
        alpha = jnp.exp(m_old - m_new)
        p = jnp.exp(s_ref[...] - (m_new - tile_const(j)))
        l_ref[...] = alpha * l_ref[...] + jnp.sum(p, axis=1, keepdims=True)
        acc_ref[...] = alpha * acc_ref[...] + jnp.dot(p.astype(BF16), v_ref[pl.ds(k0, tk), :],
                                                      preferred_element_type=F32)
        m_ref[...] = m_new

    scores(0, sa_ref, mxa_ref)
    if nk > 1:
        def pair(i, carry):
            j = 2 * i
            accumulate(j, sa_ref, mxa_ref)
            scores(j + 1, sb_ref, mxb_ref)
            accumulate(j + 1, sb_ref, mxb_ref)
            scores(j + 2, sa_ref, mxa_ref)
            return carry
        lax.fori_loop(0, nk // 2 - 1, pair, 0)
        accumulate(nk - 2, sa_ref, mxa_ref)
        scores(nk - 1, sb_ref, mxb_ref)
        accumulate(nk - 1, sb_ref, mxb_ref)
    else:
        accumulate(0, sa_ref, mxa_ref)

    lp = lamp_ref[...]
    lam = (jnp.exp(jnp.sum(lp[0:1] * lp[1:2], axis=1, keepdims=True))
           - jnp.exp(jnp.sum(lp[2:3] * lp[3:4], axis=1, keepdims=True)) + sc_ref[ATT_HEADS])
    acc = acc_ref[...]
    l = l_ref[...]
    o = acc[0:tq] / l[0:tq] - lam * (acc[tq:rows] / l[tq:rows])
    ms =jnp.mean(o * o, axis=-1, keepdims=True)
    o_ref[...] = (o * lax.rsqrt(ms + LN_EPS) * g_ref[...] * sc_ref[ATT_HEADS + 1]).astype(o_ref.dtype)


def _attention(scal, lam_params, qkv, norm_g, *, row_off, batch, seq, n_rows, tq=256, tk=1024):
    tk = min(tk, seq)
    qb0 = row_off // tq
    kb0 = row_off // seq
    nqb = seq // tq
    return pl.pallas_call(
        functools.partial(_attn_kernel, tq=tq, tk=tk, seq=seq),
        grid=(batch, ATT_HEADS, nqb),
        in_specs=[
            pl.BlockSpec(memory_space=pltpu.SMEM),
            pl.BlockSpec((4, ATT_DK), lambda b, h, i: (0, 0)),
            pl.BlockSpec((tq, HEAD_W), lambda b, h, i: (qb0 + b * nqb + i, h)),
            pl.BlockSpec((seq, HEAD_W), lambda b, h, i: (kb0 + b, ATT_HEADS + h)),
            pl.BlockSpec((seq, HEAD_W), lambda b, h, i: (kb0 + b, 2 * ATT_HEADS + h)),
            pl.BlockSpec((1, HEAD_W), lambda b, h, i: (0, 0)),
        ],
        out_specs=pl.BlockSpec((tq, HEAD_W), lambda b, h, i: (b * nqb + i, h)),
        out_shape=jax.ShapeDtypeStruct((n_rows, ATT_HEADS * HEAD_W), BF16),
        scratch_shapes=[pltpu.VMEM((2 * tq, HEAD_W), BF16),
                        pltpu.VMEM((2 + tk // tq, 2 * tq, tk), F32),
                        pltpu.VMEM((2 * tq, tk), F32), pltpu.VMEM((2 * tq, tk), F32),
                        pltpu.VMEM((2 * tq, 1), F32), pltpu.VMEM((2 * tq, 1), F32),
                        pltpu.VMEM((2 * tq, 1), F32), pltpu.VMEM((2 * tq, 1), F32),
                        pltpu.VMEM((2 * tq, HEAD_W), F32)],
        compiler_params=_params(("parallel", "parallel", "arbitrary")),
        name="diff_attention",
    )(scal, lam_params, qkv, qkv, qkv, norm_g)


def _s5_kernel(u_ref, wb_ref, wcr_ref, wci_ref, ar_ref, ai_ref, y_ref,
               br_ref, bi_ref, xr_ref, xi_ref, *, tc):
    half = S5_STATE // S5_BLOCKS

    @pl.when(pl.program_id(0) == 0)
    def _():
        xr_ref[...] = jnp.zeros(xr_ref.shape, F32)
        xi_ref[...] = jnp.zeros(xi_ref.shape, F32)

    for j in range(S5_BLOCKS):
        bu = jnp.dot(u_ref[:, 256 * j:256 * (j + 1)], wb_ref[j], preferred_element_type=F32)
        br_ref[:, half * j:half * (j + 1)] = bu[:, 0:half]
        bi_ref[:, half * j:half * (j + 1)] = bu[:, half:2 * half]

    def body(t, carry):
        xr, xi = carry
        r0 = pl.multiple_of(t * S5_SLOTS, S5_SLOTS)
        ar = ar_ref[...]
        ai = ai_ref[...]
        nxr = ar * xr - ai * xi + br_ref[pl.ds(r0, S5_SLOTS), :]
        nxi = ar * xi + ai * xr + bi_ref[pl.ds(r0, S5_SLOTS), :]
        br_ref[pl.ds(r0, S5_SLOTS), :] = nxr
        bi_ref[pl.ds(r0, S5_SLOTS), :] = nxi
        return nxr, nxi

    xr, xi = lax.fori_loop(0, tc, body, (xr_ref[...], xi_ref[...]), unroll=4)
    xr_ref[...] = xr
    xi_ref[...] = xi

    for j in range(S5_BLOCKS):
        sr = br_ref[:, half * j:half * (j + 1)].astype(BF16)
        si = bi_ref[:, half * j:half * (j + 1)].astype(BF16)
        y_ref[:, 256 * j:256 * (j + 1)] = (
            jnp.dot(sr, wcr_ref[j], preferred_element_type=F32)
            + jnp.dot(si, wci_ref[j], preferred_element_type=F32))


def _s5_scan(u8, wb, wcr, wci, ar8, ai8, *, seq, tc=128):
    rows = tc * S5_SLOTS
    return pl.pallas_call(
        functools.partial(_s5_kernel, tc=tc),
        grid=(seq // tc,),
        in_specs=[pl.BlockSpec((rows, 2 * S5_W), lambda n: (n, 0)),
                  _const_spec((S5_BLOCKS, 256, 1024)), _const_spec((S5_BLOCKS, 512, 256)),
                  _const_spec((S5_BLOCKS, 512, 256)),
                  _const_spec((S5_SLOTS, S5_STATE)), _const_spec((S5_SLOTS, S5_STATE))],
        out_specs=pl.BlockSpec((rows, 2 * S5_W), lambda n: (n, 0)),
        out_shape=jax.ShapeDtypeStruct((seq * S5_SLOTS, 2 * S5_W), F32),
        scratch_shapes=[pltpu.VMEM((rows, S5_STATE), F32), pltpu.VMEM((rows, S5_STATE), F32),
                        pltpu.VMEM((S5_SLOTS, S5_STATE), F32), pltpu.VMEM((S5_SLOTS, S5_STATE), F32)],
        compiler_params=_params(("arbitrary",)),
        name="s5_scan",
    )(u8, wb, wcr, wci, ar8, ai8)


def _s5_weights(a_re, a_im, log_step, b_re, b_im, c_re, c_im):
    dt = jnp.exp(log_step)[..., None]
    mag = jnp.exp(a_re * dt)
    abr = mag * jnp.cos(a_im * dt)
    abi = mag * jnp.sin(a_im * dt)
    den = a_re * a_re + a_im * a_im
    nr = abr - 1.0
    fr = (nr * a_re + abi * a_im) / den
    fi = (abi * a_re - nr * a_im) / den
    bbr = fr[..., None] * b_re - fi[..., None] * b_im
    bbi = fr[..., None] * b_im + fi[..., None] * b_re
    eye = jnp.eye(8, dtype=F32)

    def in_block(bb):
        bb = bb.reshape(2, S5_BLOCKS, 8, S5_N, S5_GC)
        return jnp.einsum('djgnc,gh->jdgchn', bb, eye).reshape(S5_BLOCKS, 256, 512)

    wb = jnp.concatenate([in_block(bbr), in_block(bbi)], axis=-1).astype(BF16)

    def out_block(cc):
        cc = cc.reshape(2, S5_BLOCKS, 8, S5_GC, S5_N)
        return jnp.einsum('djgcn,gh->jgndhc', cc, eye).reshape(S5_BLOCKS, 512, 256)

    wcr = out_block(c_re).astype(BF16)
    wci = out_block(-c_im).astype(BF16)
    return wb, wcr, wci, abr.reshape(2, S5_STATE), abi.reshape(2, S5_STATE)


def _s5_branch(us5, wb, wcr, wci, abr, abi, *, batch, seq):
    assert 2 * batch <= S5_SLOTS
    u = us5.astype(BF16).reshape(batch, seq, S5_BLOCKS, 128)
    zero = jnp.zeros_like(u)
    fwd = jnp.concatenate([u, zero], axis=-1)
    bwd = jnp.concatenate([zero, u[:, ::-1]], axis=-1)
    pad = jnp.zeros((S5_SLOTS - 2 * batch,) + fwd.shape[1:], BF16)
    u8 = jnp.concatenate([fwd, bwd, pad], axis=0)
    u8 = u8.transpose(1, 0, 2, 3).reshape(seq * S5_SLOTS, 2 * S5_W)
    slot_dir = jnp.array([0] * batch + [1] * batch, jnp.int32)
    live = jnp.concatenate([jnp.ones((2 * batch, 1), F32),
                            jnp.zeros((S5_SLOTS - 2 * batch, 1), F32)], axis=0)
    sel = jnp.concatenate([slot_dir, jnp.zeros((S5_SLOTS - 2 * batch,), jnp.int32)])
    ar8 = abr[sel] * live
    ai8 = abi[sel] * live
    y8 = _s5_scan(u8, wb, wcr, wci, ar8, ai8, seq=seq)
    y8 = y8.reshape(seq, S5_SLOTS, S5_BLOCKS, 2, 128)
    yf = y8[:, 0:batch, :, 0, :].transpose(1, 0, 2, 3).reshape(batch * seq, S5_W)
    yb = y8[::-1, batch:2 * batch, :, 1, :].transpose(1, 0, 2, 3).reshape(batch * seq, S5_W)
    return yf, yb


def _merge_kernel(x_ref, att_ref, sgu_ref, us5_ref, yf_ref, yb_ref, gate_ref,
                  d_ref, gw_ref, gb_ref, wbr_ref, wo_ref, lg_ref, lb_ref, o_ref):
    y = _gelu(us5_ref[...] * d_ref[...] + yf_ref[...] + yb_ref[...])
    glu = _sigmoid(jnp.dot(y.astype(BF16), gw_ref[...], preferred_element_type=F32) + gb_ref[...])
    s5 = (y * glu).astype(BF16)
    branches = (att_ref[...], sgu_ref[...], s5)
    mix = None
    for i in range(N_BRANCH):
        proj = jnp.dot(branches[i], wbr_ref[i], preferred_element_type=F32)
        term = gate_ref[:, i * D_MODEL:(i + 1) * D_MODEL] * proj
        mix = term if mix is None else mix + term
    mixed = jnp.dot(mix.astype(BF16), wo_ref[...], preferred_element_type=F32)
    o_ref[...] = _layernorm(ALPHA * x_ref[...] + mixed, lg_ref[...], lb_ref[...])


def _merge(x, att, sgu, us5, yf, yb, gates, s5_d, glu_w, glu_b, w_branch, w_o, ln_g, ln_b, *, tm=512):
    n = x.shape[0]
    row = lambda w: pl.BlockSpec((tm, w), lambda i: (i, 0))
    return pl.pallas_call(
        _merge_kernel,
        grid=(n // tm,),
        in_specs=[row(D_MODEL), row(S5_W), row(S5_W), row(S5_W), row(S5_W), row(S5_W),
                  row(N_BRANCH * D_MODEL),
                  _const_spec((1, S5_W)), _const_spec((S5_W, S5_W)), _const_spec((1, S5_W)),
                  _const_spec((N_BRANCH, S5_W, D_MODEL)), _const_spec((D_MODEL, D_MODEL)),
                  _const_spec((1, D_MODEL)), _const_spec((1, D_MODEL))],
        out_specs=row(D_MODEL),
        out_shape=jax.ShapeDtypeStruct((n, D_MODEL), F32),
        compiler_params=_params(("parallel",)),
        name="merge",
    )(x, att, sgu, us5, yf, yb, gates, s5_d, glu_w, glu_b, w_branch, w_o, ln_g, ln_b)


def _top_rows(s, k):
    row = lax.broadcasted_iota(jnp.int32, (k, s.shape[1]), 0)
    out = jnp.zeros((k, s.shape[1]), F32)
    for i in range(k):
        m = jnp.max(s, axis=0, keepdims=True)
        out = jnp.where(row == i, m, out)
        s = jnp.where(s == m, -jnp.inf, s)
    return out


def _pair_candidates(v0, v1):
    r8 = lax.broadcasted_iota(jnp.int32, (8, v0.shape[1]), 0)
    r16 = lax.broadcasted_iota(jnp.int32, v0.shape, 0)
    ninf = -jnp.inf
    lo = v0[0:8]
    return jnp.concatenate([
        v0[0:1] + v1,
        v0[1:2] + v1[0:8],
        jnp.where(r16 >= 2, v0 + v1[0:1], ninf),
        jnp.where(r8 >= 2, lo + v1[1:2], ninf),
        jnp.where((r8 >= 2) & (r8 <= 4), lo + v1[2:3], ninf),
        jnp.where((r8 >= 2) & (r8 <= 3), lo + v1[3:4], ninf),
        jnp.where(r8 == 2, lo + v1[4:5], ninf),
    ], axis=0)


PEER_LANES = 128
PEER_ROWS = 64
PEER_RGROUP = 4


def _peer_kernel(x_ref, wq_ref, keys_ref, u_ref, vt_ref, lg_ref, lb_ref, o_ref,
                 xb_ref, thr_ref, e0_ref, s1_ref, e1_ref, gate_ref, acc_ref, *, tm, rpc):
    c = pl.program_id(1)
    strips = [slice(i * PEER_LANES, (i + 1) * PEER_LANES) for i in range(tm // PEER_LANES)]

    def gen_gate(r0):
        for si, ls in enumerate(strips):
            for k0 in range(0, PEER_NKEYS, PEER_ROWS):
                sums = [None] * PEER_RGROUP
                for h in range(PEER_HEADS):
                    s1 = s1_ref[h, si, k0:k0 + PEER_ROWS, :]
                    ex = e1_ref[h, si, k0:k0 + PEER_ROWS, :]
                    for r in range(PEER_RGROUP):
                        e1 = c * rpc + r0 + r
                        t = thr_ref[h, si, pl.ds(e1, 1), :]
                        w = e0_ref[h, si, pl.ds(e1, 1), :]
                        term = jnp.where(s1 >= t, ex, 0.0) * w
                        sums[r] = term if sums[r] is None else sums[r] + term
                for r in range(PEER_RGROUP):
                    g0 = (r0 + r) * PEER_NKEYS + k0
                    gate_ref[g0:g0 + PEER_ROWS, ls] = sums[r]

    @pl.when(c == 0)
    def _():
        xb = x_ref[...].astype(BF16)
        xb_ref[...] = xb
        q = jnp.dot(xb, wq_ref[...], preferred_element_type=F32).astype(BF16)
        for h in range(PEER_HEADS):
            st = []
            for half in range(2):
                c0 = (2 * h + half) * PEER_DHALF
                st.append(lax.dot_general(keys_ref[h, half], q[:, c0:c0 + PEER_DHALF],
                                          (((1,), (1,)), ((), ())), preferred_element_type=F32))
            for si, ls in enumerate(strips):
                a = st[0][:, ls]
                b = st[1][:, ls]
                v0 = _top_rows(a, PEER_TOPK)
                v1 = _top_rows(b, PEER_TOPK)
                cs = _top_rows(_pair_candidates(v0, v1), PEER_TOPK)
                tau = cs[PEER_TOPK - 1:PEER_TOPK]
                z = jnp.sum(jnp.exp(cs - cs[0:1]), axis=0, keepdims=True)
                thr_ref[h, si] = tau - a
                e0_ref[h, si] = jnp.exp(a - v0[0:1]) / z
                s1_ref[h, si] = b
                e1_ref[h, si] = jnp.exp(b - v1[0:1])
        acc_ref[...] = jnp.zeros(acc_ref.shape, F32)

    pe = PEER_RGROUP * PEER_NKEYS
    acts = []
    for r0 in range(0, rpc, PEER_RGROUP):
        p0 = r0 * PEER_NKEYS
        hid = lax.dot_general(u_ref[p0:p0 + pe, :], xb_ref[...], (((1,), (1,)), ((), ())),
                              preferred_element_type=F32)
        gen_gate(r0)
        acts.append((gate_ref[p0:p0 + pe, :] * _gelu(hid)).astype(BF16))
    acc_ref[...] += jnp.dot(vt_ref[...], jnp.concatenate(acts, axis=0),
                            preferred_element_type=F32)

    @pl.when(c == pl.num_programs(1) - 1)
    def _():
        out = acc_ref[...].T
        o_ref[...] = _layernorm(ALPHA * x_ref[...] + out, lg_ref[...], lb_ref[...])


def _peer(x, w_q, keys, u_tab, v_tab_t, ln_g, ln_b, *, tm=512, rpc=8):
    n = x.shape[0]
    ec = rpc * PEER_NKEYS
    assert rpc % PEER_RGROUP == 0 and tm % PEER_LANES == 0
    return pl.pallas_call(
        functools.partial(_peer_kernel, tm=tm, rpc=rpc),
        grid=(n // tm, PEER_EXPERTS // ec),
        in_specs=[pl.BlockSpec((tm, D_MODEL), lambda i, c: (i, 0)),
                  _const_spec((D_MODEL, PEER_HEADS * PEER_DQ)),
                  _const_spec((PEER_HEADS, 2, PEER_NKEYS, PEER_DHALF)),
                  pl.BlockSpec((ec, D_MODEL), lambda i, c: (c, 0)),
                  pl.BlockSpec((D_MODEL, ec), lambda i, c: (0, c)),
                  _const_spec((1, D_MODEL)), _const_spec((1, D_MODEL))],
        out_specs=pl.BlockSpec((tm, D_MODEL), lambda i, c: (i, 0)),
        out_shape=jax.ShapeDtypeStruct((n, D_MODEL), F32),
        scratch_shapes=[pltpu.VMEM((tm, D_MODEL), BF16),
                        pltpu.VMEM((PEER_HEADS, tm // PEER_LANES, PEER_NKEYS, PEER_LANES), F32),
                        pltpu.VMEM((PEER_HEADS, tm // PEER_LANES, PEER_NKEYS, PEER_LANES), F32),
                        pltpu.VMEM((PEER_HEADS, tm // PEER_LANES, PEER_NKEYS, PEER_LANES), F32),
                        pltpu.VMEM((PEER_HEADS, tm // PEER_LANES, PEER_NKEYS, PEER_LANES), F32),
                        pltpu.VMEM((ec, tm), F32),
                        pltpu.VMEM((D_MODEL, tm), F32)],
        compiler_params=_params(("parallel", "arbitrary")),
        name="peer",
    )(x, w_q, keys, u_tab, v_tab_t, ln_g, ln_b)


def _layer(x, p, trunks):
    n = x.shape[0]
    qkv, sgu, us5, gates = _in_proj(x, p['w_in'], p['b_gate'], p['sgu_ln_g'], p['sgu_ln_b'],
                                    p['sgu_w_s'], p['sgu_b_s'])
    wb, wcr, wci, abr, abi = _s5_weights(p['s5_a_re'], p['s5_a_im'], p['s5_log_step'],
                                         p['s5_b_re'], p['s5_b_im'], p['s5_c_re'], p['s5_c_im'])
    att, yf, yb = [], [], []
    for row_off, batch, seq in trunks:
        nr = batch * seq
        att.append(_attention(p['att_scal'], p['lam_params'], qkv, p['att_norm_g'],
                              row_off=row_off, batch=batch, seq=seq, n_rows=nr))
        f, b = _s5_branch(lax.slice_in_dim(us5, row_off, row_off + nr, axis=0),
                          wb, wcr, wci, abr, abi, batch=batch, seq=seq)
        yf.append(f)
        yb.append(b)
    att = jnp.concatenate(att, axis=0)
    yf = jnp.concatenate(yf, axis=0)
    yb = jnp.concatenate(yb, axis=0)
    x = _merge(x, att, sgu, us5, yf, yb, gates, p['s5_d'], p['s5_glu_w'], p['s5_glu_b'],
               p['w_branch'], p['w_o'], p['ln1_g'], p['ln1_b'])
    x = _peer(x, p['peer_w_q'], p['peer_keys'], p['peer_u'], p['peer_v_t'], p['ln2_g'], p['ln2_b'])
    return x


def _prepare(w_in, b_gate, lambda_q1, lambda_k1, lambda_q2, lambda_k2, att_norm_g,
             sgu_ln_g, sgu_ln_b, sgu_w_s, sgu_b_s,
             s5_a_re, s5_a_im, s5_log_step, s5_b_re, s5_b_im, s5_c_re, s5_c_im, s5_d, s5_glu_w, s5_glu_b,
             w_branch, w_o, ln1_g, ln1_b, peer_w_q, peer_keys, peer_u, peer_v, ln2_g, ln2_b):
    depth = w_in.shape[0]
    lam_init = 0.8 - 0.6 * jnp.exp(-0.3 * jnp.arange(depth, dtype=F32))
    slopes = jnp.exp2(-8.0 * jnp.arange(1, ATT_HEADS + 1, dtype=F32) / ATT_HEADS)
    att_scal = jnp.concatenate([jnp.broadcast_to(slopes, (depth, ATT_HEADS)),
                                lam_init[:, None], 1.0 - lam_init[:, None]], axis=1)
    row = lambda a: a.reshape(depth, 1, -1).astype(F32)
    return dict(
        w_in=w_in.astype(BF16), b_gate=row(b_gate),
        att_scal=att_scal,
        lam_params=jnp.stack([lambda_q1, lambda_k1, lambda_q2, lambda_k2], axis=1).astype(F32),
        att_norm_g=row(att_norm_g),
        sgu_ln_g=row(sgu_ln_g), sgu_ln_b=row(sgu_ln_b), sgu_w_s=sgu_w_s.astype(BF16),
        sgu_b_s=jnp.repeat(jnp.swapaxes(sgu_b_s, 1, 2), SGU_W // SGU_GROUPS, axis=2).astype(F32),
        s5_a_re=s5_a_re, s5_a_im=s5_a_im, s5_log_step=s5_log_step, s5_b_re=s5_b_re, s5_b_im=s5_b_im,
        s5_c_re=s5_c_re, s5_c_im=s5_c_im,
        s5_d=row(s5_d), s5_glu_w=s5_glu_w.astype(BF16), s5_glu_b=row(s5_glu_b),
        w_branch=w_branch.astype(BF16), w_o=w_o.astype(BF16), ln1_g=row(ln1_g), ln1_b=row(ln1_b),
        peer_w_q=peer_w_q.astype(BF16), peer_keys=peer_keys.astype(BF16),
        peer_u=peer_u.astype(BF16), peer_v_t=jnp.swapaxes(peer_v.astype(BF16), 1, 2),
        ln2_g=row(ln2_g), ln2_b=row(ln2_b),
    )


def _trunk(xs, weights):
    trunks, off = [], 0
    for x in xs:
        b, s, _ = x.shape
        trunks.append((off, b, s))
        off += b * s
    x = jnp.concatenate([x.reshape(-1, D_MODEL) for x in xs], axis=0)
    params = _prepare(*weights)
    trunks = tuple(trunks)
    x, _ = lax.scan(lambda carry, p: (_layer(carry, p, trunks), None), x, params)
    outs = []
    for (off, b, s), xi in zip(trunks, xs):
        outs.append(x[off:off + b * s].reshape(xi.shape))
    return tuple(outs)


def kernel(x_prompt, x_sample, w_in, b_gate, lambda_q1, lambda_k1, lambda_q2, lambda_k2, att_norm_g, sgu_ln_g, sgu_ln_b, sgu_w_s, sgu_b_s, s5_a_re, s5_a_im, s5_log_step, s5_b_re, s5_b_im, s5_c_re, s5_c_im, s5_d, s5_glu_w, s5_glu_b, w_branch, w_o, ln1_g, ln1_b, peer_w_q, peer_keys, peer_u, peer_v, ln2_g, ln2_b):
    weights = (w_in, b_gate, lambda_q1, lambda_k1, lambda_q2, lambda_k2, att_norm_g,
               sgu_ln_g, sgu_ln_b, sgu_w_s, sgu_b_s,
               s5_a_re, s5_a_im, s5_log_step, s5_b_re, s5_b_im, s5_c_re, s5_c_im, s5_d, s5_glu_w, s5_glu_b,
               w_branch, w_o, ln1_g, ln1_b, peer_w_q, peer_keys, peer_u, peer_v, ln2_g, ln2_b)
    y_prompt, y_sample = _trunk([x_prompt, x_sample], weights)
    return (y_prompt, y_sample)
```

```python
import functools
import math

import jax
import jax.numpy as jnp
from jax import lax
from jax.experimental import pallas as pl
from jax.experimental.pallas import tpu as pltpu

F32 = jnp.float32
BF16 = jnp.bfloat16

D_MODEL = 1024
DEPTH = 4
ATT_HEADS = 4
ATT_DK = 64
HEAD_W = 2 * ATT_DK
QKV_W = 3 * ATT_HEADS * HEAD_W
SGU_GROUPS = 4
SGU_CHUNK = 128
SGU_W = 512
S5_GC = 16
S5_GROUPS = 32
S5_N = 64
S5_W = 512
S5_STATE = S5_GROUPS * S5_N
S5_SLOTS = 8
S5_BLOCKS = 4
N_BRANCH = 3
Z_OFF = QKV_W
U_OFF = Z_OFF + 2 * SGU_W
G_OFF = U_OFF + S5_W
IN_W = G_OFF + N_BRANCH * D_MODEL
PEER_HEADS = 8
PEER_NKEYS = 128
PEER_EXPERTS = PEER_NKEYS * PEER_NKEYS
PEER_DQ = 256
PEER_DHALF = 128
PEER_TOPK = 16
ALPHA = (2 * DEPTH) ** 0.25
LN_EPS = 1e-5
NEG_BIG = -1e30

VMEM_LIMIT = 56 * 1024 * 1024


GELU_C1 = 0.7978845608028654
GELU_C3 = GELU_C1 * 0.044715


def _gelu(x):
    return 0.5 * x * (1.0 + jnp.tanh(x * (GELU_C1 + GELU_C3 * (x * x))))


def _sigmoid(x):
    return 1.0 / (1.0 + jnp.exp(-x))


def _layernorm(x, g, b):
    mu = jnp.mean(x, axis=-1, keepdims=True)
    xc = x - mu
    var = jnp.mean(xc * xc, axis=-1, keepdims=True)
    return xc * lax.rsqrt(var + LN_EPS) * g + b


def _const_spec(shape):
    nd = len(shape)
    return pl.BlockSpec(shape, lambda *_: (0,) * nd, pipeline_mode=pl.Buffered(1))


def _params(sem, flags=None):
    return pltpu.CompilerParams(dimension_semantics=sem, vmem_limit_bytes=VMEM_LIMIT, flags=flags)


def _inproj_kernel(x_ref, w_ref, bg_ref, lng_ref, lnb_ref, ws_ref, bs_ref,
                   qkv_ref, sgu_ref, us5_ref, gate_ref, *, tm):
    xb = x_ref[...].astype(BF16)

    h = jnp.dot(xb, w_ref[:, 0:QKV_W], preferred_element_type=F32)
    nq = ATT_HEADS * HEAD_W
    qkv_ref[:, 0:nq] = (h[:, 0:nq] * (ATT_DK ** -0.5)).astype(BF16)
    qkv_ref[:, nq:QKV_W] = h[:, nq:QKV_W].astype(BF16)

    z = _gelu(jnp.dot(xb, w_ref[:, Z_OFF:U_OFF], preferred_element_type=F32))
    u = z[:, 0:SGU_W]
    v = _layernorm(z[:, SGU_W:2 * SGU_W], lng_ref[...], lnb_ref[...]).astype(BF16)
    for ci in range(tm // SGU_CHUNK):
        r0 = ci * SGU_CHUNK
        for g in range(SGU_GROUPS):
            c0 = g * 128
            sv = jnp.dot(ws_ref[g], v[r0:r0 + SGU_CHUNK, c0:c0 + 128],
                         preferred_element_type=F32) + bs_ref[:, c0:c0 + 128]
            sgu_ref[r0:r0 + SGU_CHUNK, c0:c0 + 128] = (
                u[r0:r0 + SGU_CHUNK, c0:c0 + 128] * sv).astype(BF16)

    us5_ref[...] = jnp.dot(xb, w_ref[:, U_OFF:G_OFF], preferred_element_type=F32)

    gl = jnp.dot(xb, w_ref[:, G_OFF:IN_W], preferred_element_type=F32)
    gate_ref[...] = _sigmoid(gl + bg_ref[...])


def _in_proj(x, w_in, b_gate, ln_g, ln_b, w_s, b_s_full, *, tm=256):
    n = x.shape[0]
    row = lambda w: pl.BlockSpec((tm, w), lambda i: (i, 0))
    return pl.pallas_call(
        functools.partial(_inproj_kernel, tm=tm),
        grid=(n // tm,),
        in_specs=[row(D_MODEL), _const_spec((D_MODEL, IN_W)), _const_spec((1, N_BRANCH * D_MODEL)),
                  _const_spec((1, SGU_W)), _const_spec((1, SGU_W)),
                  _const_spec((SGU_GROUPS, SGU_CHUNK, SGU_CHUNK)), _const_spec((SGU_CHUNK, SGU_W))],
        out_specs=[row(QKV_W), row(SGU_W), row(S5_W), row(N_BRANCH * D_MODEL)],
        out_shape=[jax.ShapeDtypeStruct((n, QKV_W), BF16), jax.ShapeDtypeStruct((n, SGU_W), BF16),
                   jax.ShapeDtypeStruct((n, S5_W), F32),
                   jax.ShapeDtypeStruct((n, N_BRANCH * D_MODEL), F32)],
        compiler_params=_params(("parallel",)),
        name="in_proj",
    )(x, w_in, b_gate, ln_g, ln_b, w_s, b_s_full)


def _attn_kernel(sc_ref, lamp_ref, q_ref, k_ref, v_ref, g_ref, o_ref,
                 qs_ref, bias_ref, va_ref, sa_ref, sb_ref, mxa_ref, mxb_ref, m_ref, acc_ref,
                 *, tq, tk, seq):
    hd = pl.program_id(1)
    qi = pl.program_id(2)
    slope = sc_ref[hd]
    nk = seq // tk
    nd = tk // tq
    jd = (qi * tq) // tk
    rows = 2 * tq
    assert nk == 1 or nk % 2 == 0

    @pl.when(qi == 0)
    def _():
        ri = lax.broadcasted_iota(jnp.int32, (rows, tk), 0)
        ri = jnp.where(ri >= tq, ri - tq, ri)
        ci = lax.broadcasted_iota(jnp.int32, (rows, tk), 1)
        rel = (ri - ci).astype(F32)
        bias_ref[0] = slope * rel
        bias_ref[1] = -slope * rel
        for v in range(nd):
            bias_ref[2 + v] = slope * jnp.abs(rel + float(v * tq))
        va_ref[:, 0:HEAD_W] = v_ref[...]
        va_ref[:, HEAD_W:2 * HEAD_W] = jnp.ones((seq, HEAD_W), BF16)

    q = q_ref[...]
    lane = lax.broadcasted_iota(jnp.int32, q.shape, 1)
    zero = jnp.zeros_like(q)
    qs_ref[0:tq, :] = jnp.where(lane < ATT_DK, q, zero)
    qs_ref[tq:rows, :] = jnp.where(lane >= ATT_DK, q, zero)

    m_ref[...] = jnp.full(m_ref.shape, NEG_BIG, F32)
    acc_ref[...] = jnp.zeros(acc_ref.shape, F32)

    def tile_const(j):
        off = jnp.abs(qi * tq - j * tk).astype(F32)
        return jnp.where(j == jd, 0.0, -slope * off)

    def key_start(j):
        return j * tk if isinstance(j, int) else pl.multiple_of(j * tk, tk)

    def scores(j, s_ref, mx_ref):
        k0 = key_start(j)
        s = lax.dot_general(qs_ref[...], k_ref[pl.ds(k0, tk), :], (((1,), (1,)), ((), ())),
                            preferred_element_type=F32)
        typ = jnp.where(j < jd, 0, jnp.where(j > jd, 1, 2 + qi % nd))
        s = s - bias_ref[typ]
        s_ref[...] = s
        mx_ref[...] = jnp.max(s, axis=1, keepdims=True) + tile_const(j)

    def accumulate(j, s_ref, mx_ref):
        k0 = key_start(j)
        m_old = m_ref[...]
        m_new = jnp.maximum(m_old, mx_ref[...])
        alpha = jnp.exp(m_old - m_new)
        p = jnp.exp(s_ref[...] - (m_new - tile_const(j))).astype(BF16)
        acc_ref[...] = alpha * acc_ref[...] + jnp.dot(p, va_ref[pl.ds(k0, tk), :],
                                                      preferred_element_type=F32)
        m_ref[...] = m_new

    scores(0, sa_ref, mxa_ref)
    if nk > 1:
        def pair(i, carry):
            j = 2 * i
            accumulate(j, sa_ref, mxa_ref)
            scores(j + 1, sb_ref, mxb_ref)
            accumulate(j + 1, sb_ref, mxb_ref)
            scores(j + 2, sa_ref, mxa_ref)
            return carry
        lax.fori_loop(0, nk // 2 - 1, pair, 0)
        accumulate(nk - 2, sa_ref, mxa_ref)
        scores(nk - 1, sb_ref, mxb_ref)
        accumulate(nk - 1, sb_ref, mxb_ref)
    else:
        accumulate(0, sa_ref, mxa_ref)

    lp = lamp_ref[...]
    lam = (jnp.exp(jnp.sum(lp[0:1] * lp[1:2], axis=1, keepdims=True))
           - jnp.exp(jnp.sum(lp[2:3] * lp[3:4], axis=1, keepdims=True)) + sc_ref[ATT_HEADS])
    num = acc_ref[:, 0:HEAD_W]
    den = acc_ref[:, HEAD_W:2 * HEAD_W]
    o = num[0:tq] / den[0:tq] - lam * (num[tq:rows] / den[tq:rows])
    ms = jnp.mean(o * o, axis=-1, keepdims=True)
    o_ref[...] = (o * lax.rsqrt(ms + LN_EPS) * g_ref[...] * sc_ref[ATT_HEADS + 1]).astype(o_ref.dtype)


def _attention(scal, lam_params, qkv, norm_g, *, row_off, batch, seq, n_rows, tq=256, tk=1024):
    tk = min(tk, seq)
    qb0 = row_off // tq
    kb0 = row_off // seq
    nqb = seq // tq
    return pl.pallas_call(
        functools.partial(_attn_kernel, tq=tq, tk=tk, seq=seq),
        grid=(batch, ATT_HEADS, nqb),
        in_specs=[
            pl.BlockSpec(memory_space=pltpu.SMEM),
            pl.BlockSpec((4, ATT_DK), lambda b, h, i: (0, 0)),
            pl.BlockSpec((tq, HEAD_W), lambda b, h, i: (qb0 + b * nqb + i, h)),
            pl.BlockSpec((seq, HEAD_W), lambda b, h, i: (kb0 + b, ATT_HEADS + h)),
            pl.BlockSpec((seq, HEAD_W), lambda b, h, i: (kb0 + b, 2 * ATT_HEADS + h)),
            pl.BlockSpec((1, HEAD_W), lambda b, h, i: (0, 0)),
        ],
        out_specs=pl.BlockSpec((tq, HEAD_W), lambda b, h, i: (b * nqb + i, h)),
        out_shape=jax.ShapeDtypeStruct((n_rows, ATT_HEADS * HEAD_W), BF16),
        scratch_shapes=[pltpu.VMEM((2 * tq, HEAD_W), BF16),
                        pltpu.VMEM((2 + tk // tq, 2 * tq, tk), F32),
                        pltpu.VMEM((seq, 2 * HEAD_W), BF16),
                        pltpu.VMEM((2 * tq, tk), F32), pltpu.VMEM((2 * tq, tk), F32),
                        pltpu.VMEM((2 * tq, 1), F32), pltpu.VMEM((2 * tq, 1), F32),
                        pltpu.VMEM((2 * tq, 1), F32),
                        pltpu.VMEM((2 * tq, 2 * HEAD_W), F32)],
        compiler_params=_params(("parallel", "parallel", "arbitrary")),
        name="diff_attention",
    )(scal, lam_params, qkv, qkv, qkv, norm_g)


def _s5_kernel(u_ref, wb_ref, wcr_ref, wci_ref, ar_ref, ai_ref, y_ref,
               br_ref, bi_ref, xr_ref, xi_ref, *, tc):
    half = S5_STATE // S5_BLOCKS

    @pl.when(pl.program_id(0) == 0)
    def _():
        xr_ref[...] = jnp.zeros(xr_ref.shape, F32)
        xi_ref[...] = jnp.zeros(xi_ref.shape, F32)

    for j in range(S5_BLOCKS):
        bu = jnp.dot(u_ref[:, 256 * j:256 * (j + 1)], wb_ref[j], preferred_element_type=F32)
        br_ref[:, half * j:half * (j + 1)] = bu[:, 0:half]
        bi_ref[:, half * j:half * (j + 1)] = bu[:, half:2 * half]

    def body(t, carry):
        xr, xi = carry
        r0 = pl.multiple_of(t * S5_SLOTS, S5_SLOTS)
        ar = ar_ref[...]
        ai = ai_ref[...]
        nxr = ar * xr - ai * xi + br_ref[pl.ds(r0, S5_SLOTS), :]
        nxi = ar * xi + ai * xr + bi_ref[pl.ds(r0, S5_SLOTS), :]
        br_ref[pl.ds(r0, S5_SLOTS), :] = nxr
        bi_ref[pl.ds(r0, S5_SLOTS), :] = nxi
        return nxr, nxi

    xr, xi = lax.fori_loop(0, tc, body, (xr_ref[...], xi_ref[...]), unroll=4)
    xr_ref[...] = xr
    xi_ref[...] = xi

    for j in range(S5_BLOCKS):
        sr = br_ref[:, half * j:half * (j + 1)].astype(BF16)
        si = bi_ref[:, half * j:half * (j + 1)].astype(BF16)
        y_ref[:, 256 * j:256 * (j + 1)] = (
            jnp.dot(sr, wcr_ref[j], preferred_element_type=F32)
            + jnp.dot(si, wci_ref[j], preferred_element_type=F32))


def _s5_scan(u8, wb, wcr, wci, ar8, ai8, *, seq, tc=128):
    rows = tc * S5_SLOTS
    return pl.pallas_call(
        functools.partial(_s5_kernel, tc=tc),
        grid=(seq // tc,),
        in_specs=[pl.BlockSpec((rows, 2 * S5_W), lambda n: (n, 0)),
                  _const_spec((S5_BLOCKS, 256, 1024)), _const_spec((S5_BLOCKS, 512, 256)),
                  _const_spec((S5_BLOCKS, 512, 256)),
                  _const_spec((S5_SLOTS, S5_STATE)), _const_spec((S5_SLOTS, S5_STATE))],
        out_specs=pl.BlockSpec((rows, 2 * S5_W), lambda n: (n, 0)),
        out_shape=jax.ShapeDtypeStruct((seq * S5_SLOTS, 2 * S5_W), F32),
        scratch_shapes=[pltpu.VMEM((rows, S5_STATE), F32), pltpu.VMEM((rows, S5_STATE), F32),
                        pltpu.VMEM((S5_SLOTS, S5_STATE), F32), pltpu.VMEM((S5_SLOTS, S5_STATE), F32)],
        compiler_params=_params(("arbitrary",)),
        name="s5_scan",
    )(u8, wb, wcr, wci, ar8, ai8)


def _s5_weights(a_re, a_im, log_step, b_re, b_im, c_re, c_im):
    dt = jnp.exp(log_step)[..., None]
    mag = jnp.exp(a_re * dt)
    abr = mag * jnp.cos(a_im * dt)
    abi = mag * jnp.sin(a_im * dt)
    den = a_re * a_re + a_im * a_im
    nr = abr - 1.0
    fr = (nr * a_re + abi * a_im) / den
    fi = (abi * a_re - nr * a_im) / den
    bbr = fr[..., None] * b_re - fi[..., None] * b_im
    bbi = fr[..., None] * b_im + fi[..., None] * b_re
    eye = jnp.eye(8, dtype=F32)

    def in_block(bb):
        bb = bb.reshape(2, S5_BLOCKS, 8, S5_N, S5_GC)
        return jnp.einsum('djgnc,gh->jdgchn', bb, eye).reshape(S5_BLOCKS, 256, 512)

    wb = jnp.concatenate([in_block(bbr), in_block(bbi)], axis=-1).astype(BF16)

    def out_block(cc):
        cc = cc.reshape(2, S5_BLOCKS, 8, S5_GC, S5_N)
        return jnp.einsum('djgcn,gh->jgndhc', cc, eye).reshape(S5_BLOCKS, 512, 256)

    wcr = out_block(c_re).astype(BF16)
    wci = out_block(-c_im).astype(BF16)
    return wb, wcr, wci, abr.reshape(2, S5_STATE), abi.reshape(2, S5_STATE)


def _s5_branch(us5, wb, wcr, wci, abr, abi, *, batch, seq):
    assert 2 * batch <= S5_SLOTS
    u = us5.astype(BF16).reshape(batch, seq, S5_BLOCKS, 128)
    zero = jnp.zeros_like(u)
    fwd = jnp.concatenate([u, zero], axis=-1)
    bwd = jnp.concatenate([zero, u[:, ::-1]], axis=-1)
    pad = jnp.zeros((S5_SLOTS - 2 * batch,) + fwd.shape[1:], BF16)
    u8 = jnp.concatenate([fwd, bwd, pad], axis=0)
    u8 = u8.transpose(1, 0, 2, 3).reshape(seq * S5_SLOTS, 2 * S5_W)
    slot_dir = jnp.array([0] * batch + [1] * batch, jnp.int32)
    live = jnp.concatenate([jnp.ones((2 * batch, 1), F32),
                            jnp.zeros((S5_SLOTS - 2 * batch, 1), F32)], axis=0)
    sel = jnp.concatenate([slot_dir, jnp.zeros((S5_SLOTS - 2 * batch,), jnp.int32)])
    ar8 = abr[sel] * live
    ai8 = abi[sel] * live
    y8 = _s5_scan(u8, wb, wcr, wci, ar8, ai8, seq=seq)
    y8 = y8.reshape(seq, S5_SLOTS, S5_BLOCKS, 2, 128)
    yf = y8[:, 0:batch, :, 0, :].transpose(1, 0, 2, 3).reshape(batch * seq, S5_W)
    yb = y8[::-1, batch:2 * batch, :, 1, :].transpose(1, 0, 2, 3).reshape(batch * seq, S5_W)
    return yf, yb


def _merge_kernel(x_ref, att_ref, sgu_ref, us5_ref, yf_ref, yb_ref, gate_ref,
                  d_ref, gw_ref, gb_ref, wbr_ref, wo_ref, lg_ref, lb_ref, o_ref):
    y = _gelu(us5_ref[...] * d_ref[...] + yf_ref[...] + yb_ref[...])
    glu = _sigmoid(jnp.dot(y.astype(BF16), gw_ref[...], preferred_element_type=F32) + gb_ref[...])
    s5 = (y * glu).astype(BF16)
    branches = (att_ref[...], sgu_ref[...], s5)
    mix = None
    for i in range(N_BRANCH):
        proj = jnp.dot(branches[i], wbr_ref[i], preferred_element_type=F32)
        term = gate_ref[:, i * D_MODEL:(i + 1) * D_MODEL] * proj
        mix = term if mix is None else mix + term
    mixed = jnp.dot(mix.astype(BF16), wo_ref[...], preferred_element_type=F32)
    o_ref[...] = _layernorm(ALPHA * x_ref[...] + mixed, lg_ref[...], lb_ref[...])


def _merge(x, att, sgu, us5, yf, yb, gates, s5_d, glu_w, glu_b, w_branch, w_o, ln_g, ln_b, *, tm=512):
    n = x.shape[0]
    row = lambda w: pl.BlockSpec((tm, w), lambda i: (i, 0))
    return pl.pallas_call(
        _merge_kernel,
        grid=(n // tm,),
        in_specs=[row(D_MODEL), row(S5_W), row(S5_W), row(S5_W), row(S5_W), row(S5_W),
                  row(N_BRANCH * D_MODEL),
                  _const_spec((1, S5_W)), _const_spec((S5_W, S5_W)), _const_spec((1, S5_W)),
                  _const_spec((N_BRANCH, S5_W, D_MODEL)), _const_spec((D_MODEL, D_MODEL)),
                  _const_spec((1, D_MODEL)), _const_spec((1, D_MODEL))],
        out_specs=row(D_MODEL),
        out_shape=jax.ShapeDtypeStruct((n, D_MODEL), F32),
        compiler_params=_params(("parallel",)),
        name="merge",
    )(x, att, sgu, us5, yf, yb, gates, s5_d, glu_w, glu_b, w_branch, w_o, ln_g, ln_b)


def _top_rows(s, k, with_rank=False):
    row = lax.broadcasted_iota(jnp.int32, (k, s.shape[1]), 0)
    out = jnp.zeros((k, s.shape[1]), F32)
    rank = jnp.full(s.shape, float(k), F32)
    for i in range(k):
        m = jnp.max(s, axis=0, keepdims=True)
        out = jnp.where(row == i, m, out)
        hit = s == m
        if with_rank:
            rank = jnp.where(hit, float(i), rank)
        s = jnp.where(hit, -jnp.inf, s)
    return (out, rank) if with_rank else out


def _pair_candidates(v0, v1):
    r8 = lax.broadcasted_iota(jnp.int32, (8, v0.shape[1]), 0)
    r16 = lax.broadcasted_iota(jnp.int32, v0.shape, 0)
    ninf = -jnp.inf
    lo = v0[0:8]
    return jnp.concatenate([
        v0[0:1] + v1,
        v0[1:2] + v1[0:8],
        jnp.where(r16 >= 2, v0 + v1[0:1], ninf),
        jnp.where(r8 >= 2, lo + v1[1:2], ninf),
        jnp.where((r8 >= 2) & (r8 <= 4), lo + v1[2:3], ninf),
        jnp.where((r8 >= 2) & (r8 <= 3), lo + v1[3:4], ninf),
        jnp.where(r8 == 2, lo + v1[4:5], ninf),
    ], axis=0)


PEER_LANES = 128
PEER_PACK = 16
PEER_RGROUP = 4


def _peer_kernel(x_ref, wq_ref, keys_ref, u_ref, vt_ref, lg_ref, lb_ref, o_ref,
                 xb_ref, cnt_ref, e0_ref, rank_ref, e1_ref, gate_ref, acc_ref, *, tm, rpc):
    c = pl.program_id(1)
    strips = [slice(i * PEER_LANES, (i + 1) * PEER_LANES) for i in range(tm // PEER_LANES)]

    def gen_gate(r0):
        shape = (PEER_PACK, PEER_LANES)
        for si, ls in enumerate(strips):
            sums = [[None] * (PEER_NKEYS // PEER_PACK) for _ in range(PEER_RGROUP)]
            for h in range(PEER_HEADS):
                rk = rank_ref[h, si]
                ex = e1_ref[h, si]
                for r in range(PEER_RGROUP):
                    e1 = c * rpc + r0 + r
                    n = jnp.broadcast_to(cnt_ref[h, si, pl.ds(e1, 1), :], shape).astype(BF16)
                    w = jnp.broadcast_to(e0_ref[h, si, pl.ds(e1, 1), :], shape).astype(BF16)
                    for g in range(PEER_NKEYS // PEER_PACK):
                        rows = slice(g * PEER_PACK, (g + 1) * PEER_PACK)
                        term = jnp.where(rk[rows] < n, ex[rows], jnp.zeros(shape, BF16)) * w
                        sums[r][g] = term if sums[r][g] is None else sums[r][g] + term
            for r in range(PEER_RGROUP):
                for g in range(PEER_NKEYS // PEER_PACK):
                    g0 = (r0 + r) * PEER_NKEYS + g * PEER_PACK
                    gate_ref[g0:g0 + PEER_PACK, ls] = sums[r][g]

    @pl.when(c == 0)
    def _():
        xb = x_ref[...].astype(BF16)
        xb_ref[...] = xb
        q = jnp.dot(xb, wq_ref[...], preferred_element_type=F32).astype(BF16)
        for h in range(PEER_HEADS):
            st = []
            for half in range(2):
                c0 = (2 * h + half) * PEER_DHALF
                st.append(lax.dot_general(keys_ref[h, half], q[:, c0:c0 + PEER_DHALF],
                                          (((1,), (1,)), ((), ())), preferred_element_type=F32))
            for si, ls in enumerate(strips):
                a = st[0][:, ls]
                b = st[1][:, ls]
                v0 = _top_rows(a, PEER_TOPK)
                v1, rank1 = _top_rows(b, PEER_TOPK, with_rank=True)
                cs = _top_rows(_pair_candidates(v0, v1), PEER_TOPK)
                tau = cs[PEER_TOPK - 1:PEER_TOPK]
                z = jnp.sum(jnp.exp(cs - cs[0:1]), axis=0, keepdims=True)
                thr = tau - a
                cnt = jnp.zeros(a.shape, F32)
                for j in range(PEER_TOPK):
                    cnt = cnt + jnp.where(v1[j:j + 1] >= thr, 1.0, 0.0)
                cnt_ref[h, si] = cnt
                e0_ref[h, si] = jnp.exp(a - v0[0:1]) * (0.5 / z)
                rank_ref[h, si] = rank1.astype(BF16)
                e1_ref[h, si] = jnp.exp(b - v1[0:1]).astype(BF16)
        acc_ref[...] = jnp.zeros(acc_ref.shape, F32)

    pe = PEER_RGROUP * PEER_NKEYS
    acts = []
    for r0 in range(0, rpc, PEER_RGROUP):
        p0 = r0 * PEER_NKEYS
        hid = lax.dot_general(u_ref[p0:p0 + pe, :], xb_ref[...], (((1,), (1,)), ((), ())),
                              preferred_element_type=F32)
        gen_gate(r0)
        t = jnp.tanh(hid * (GELU_C1 + GELU_C3 * (hid * hid)))
        acts.append(((gate_ref[p0:p0 + pe, :].astype(F32) * hid) * (1.0 + t)).astype(BF16))
    acc_ref[...] += jnp.dot(vt_ref[...], jnp.concatenate(acts, axis=0),
                            preferred_element_type=F32)

    @pl.when(c == pl.num_programs(1) - 1)
    def _():
        out = acc_ref[...].T
        o_ref[...] = _layernorm(ALPHA * x_ref[...] + out, lg_ref[...], lb_ref[...])


def _peer(x, w_q, keys, u_tab, v_tab_t, ln_g, ln_b, *, tm=512, rpc=16):
    n = x.shape[0]
    ec = rpc * PEER_NKEYS
    assert rpc % PEER_RGROUP == 0 and tm % PEER_LANES == 0
    route = (PEER_HEADS, tm // PEER_LANES, PEER_NKEYS, PEER_LANES)
    return pl.pallas_call(
        functools.partial(_peer_kernel, tm=tm, rpc=rpc),
        grid=(n // tm, PEER_EXPERTS // ec),
        in_specs=[pl.BlockSpec((tm, D_MODEL), lambda i, c: (i, 0)),
                  _const_spec((D_MODEL, PEER_HEADS * PEER_DQ)),
                  _const_spec((PEER_HEADS, 2, PEER_NKEYS, PEER_DHALF)),
                  pl.BlockSpec((ec, D_MODEL), lambda i, c: (c, 0)),
                  pl.BlockSpec((D_MODEL, ec), lambda i, c: (0, c)),
                  _const_spec((1, D_MODEL)), _const_spec((1, D_MODEL))],
        out_specs=pl.BlockSpec((tm, D_MODEL), lambda i, c: (i, 0)),
        out_shape=jax.ShapeDtypeStruct((n, D_MODEL), F32),
        scratch_shapes=[pltpu.VMEM((tm, D_MODEL), BF16),
                        pltpu.VMEM(route, F32), pltpu.VMEM(route, F32),
                        pltpu.VMEM(route, BF16), pltpu.VMEM(route, BF16),
                        pltpu.VMEM((ec, tm), BF16),
                        pltpu.VMEM((D_MODEL, tm), F32)],
        compiler_params=_params(("parallel", "arbitrary")),
        name="peer",
    )(x, w_q, keys, u_tab, v_tab_t, ln_g, ln_b)


def _layer(x, p, trunks):
    n = x.shape[0]
    qkv, sgu, us5, gates = _in_proj(x, p['w_in'], p['b_gate'], p['sgu_ln_g'], p['sgu_ln_b'],
                                    p['sgu_w_s'], p['sgu_b_s'])
    wb, wcr, wci, abr, abi = _s5_weights(p['s5_a_re'], p['s5_a_im'], p['s5_log_step'],
                                         p['s5_b_re'], p['s5_b_im'], p['s5_c_re'], p['s5_c_im'])
    att, yf, yb = [], [], []
    for row_off, batch, seq in trunks:
        nr = batch * seq
        att.append(_attention(p['att_scal'], p['lam_params'], qkv, p['att_norm_g'],
                              row_off=row_off, batch=batch, seq=seq, n_rows=nr))
        f, b = _s5_branch(lax.slice_in_dim(us5, row_off, row_off + nr, axis=0),
                          wb, wcr, wci, abr, abi, batch=batch, seq=seq)
        yf.append(f)
        yb.append(b)
    att = jnp.concatenate(att, axis=0)
    yf = jnp.concatenate(yf, axis=0)
    yb = jnp.concatenate(yb, axis=0)
    x = _merge(x, att, sgu, us5, yf, yb, gates, p['s5_d'], p['s5_glu_w'], p['s5_glu_b'],
               p['w_branch'], p['w_o'], p['ln1_g'], p['ln1_b'])
    x = _peer(x, p['peer_w_q'], p['peer_keys'], p['peer_u'], p['peer_v_t'], p['ln2_g'], p['ln2_b'])
    return x


def _prepare(w_in, b_gate, lambda_q1, lambda_k1, lambda_q2, lambda_k2, att_norm_g,
             sgu_ln_g, sgu_ln_b, sgu_w_s, sgu_b_s,
             s5_a_re, s5_a_im, s5_log_step, s5_b_re, s5_b_im, s5_c_re, s5_c_im, s5_d, s5_glu_w, s5_glu_b,
             w_branch, w_o, ln1_g, ln1_b, peer_w_q, peer_keys, peer_u, peer_v, ln2_g, ln2_b):
    depth = w_in.shape[0]
    lam_init = 0.8 - 0.6 * jnp.exp(-0.3 * jnp.arange(depth, dtype=F32))
    slopes = jnp.exp2(-8.0 * jnp.arange(1, ATT_HEADS + 1, dtype=F32) / ATT_HEADS)
    att_scal = jnp.concatenate([jnp.broadcast_to(slopes, (depth, ATT_HEADS)),
                                lam_init[:, None], 1.0 - lam_init[:, None]], axis=1)
    row = lambda a: a.reshape(depth, 1, -1).astype(F32)
    return dict(
        w_in=w_in.astype(BF16), b_gate=row(b_gate),
        att_scal=att_scal,
        lam_params=jnp.stack([lambda_q1, lambda_k1, lambda_q2, lambda_k2], axis=1).astype(F32),
        att_norm_g=row(att_norm_g),
        sgu_ln_g=row(sgu_ln_g), sgu_ln_b=row(sgu_ln_b), sgu_w_s=sgu_w_s.astype(BF16),
        sgu_b_s=jnp.repeat(jnp.swapaxes(sgu_b_s, 1, 2), SGU_W // SGU_GROUPS, axis=2).astype(F32),
        s5_a_re=s5_a_re, s5_a_im=s5_a_im, s5_log_step=s5_log_step, s5_b_re=s5_b_re, s5_b_im=s5_b_im,
        s5_c_re=s5_c_re, s5_c_im=s5_c_im,
        s5_d=row(s5_d), s5_glu_w=s5_glu_w.astype(BF16), s5_glu_b=row(s5_glu_b),
        w_branch=w_branch.astype(BF16), w_o=w_o.astype(BF16), ln1_g=row(ln1_g), ln1_b=row(ln1_b),
        peer_w_q=peer_w_q.astype(BF16), peer_keys=peer_keys.astype(BF16),
        peer_u=peer_u.astype(BF16), peer_v_t=jnp.swapaxes(peer_v.astype(BF16), 1, 2),
        ln2_g=row(ln2_g), ln2_b=row(ln2_b),
    )


def _trunk(xs, weights):
    trunks, off = [], 0
    for x in xs:
        b, s, _ = x.shape
        trunks.append((off, b, s))
        off += b * s
    x = jnp.concatenate([x.reshape(-1, D_MODEL) for x in xs], axis=0)
    params = _prepare(*weights)
    trunks = tuple(trunks)
    x, _ = lax.scan(lambda carry, p: (_layer(carry, p, trunks), None), x, params)
    outs = []
    for (off, b, s), xi in zip(trunks, xs):
        outs.append(x[off:off + b * s].reshape(xi.shape))
    return tuple(outs)


def kernel(x_prompt, x_sample, w_in, b_gate, lambda_q1, lambda_k1, lambda_q2, lambda_k2, att_norm_g, sgu_ln_g, sgu_ln_b, sgu_w_s, sgu_b_s, s5_a_re, s5_a_im, s5_log_step, s5_b_re, s5_b_im, s5_c_re, s5_c_im, s5_d, s5_glu_w, s5_glu_b, w_branch, w_o, ln1_g, ln1_b, peer_w_q, peer_keys, peer_u, peer_v, ln2_g, ln2_b):
    weights = (w_in, b_gate, lambda_q1, lambda_k1, lambda_q2, lambda_k2, att_norm_g,
               sgu_ln_g, sgu_ln_b, sgu_w_s, sgu_b_s,
               s5_a_re, s5_a_im, s5_log_step, s5_b_re, s5_b_im, s5_c_re, s5_c_im, s5_d, s5_glu_w, s5_glu_b,
               w_branch, w_o, ln1_g, ln1_b, peer_w_q, peer_keys, peer_u, peer_v, ln2_g, ln2_b)
    y_prompt, y_sample = _trunk([x_prompt, x_sample], weights)
    return (y_prompt, y_sample)
```

```python
import functools
import math

import jax
import jax.numpy as jnp
from jax import lax
from jax.experimental import pallas as pl
from jax.experimental.pallas import tpu as pltpu

F32 = jnp.float32
BF16 = jnp.bfloat16

D_MODEL = 1024
DEPTH = 4
ATT_HEADS = 4
ATT_DK = 64
HEAD_W = 2 * ATT_DK
QKV_W = 3 * ATT_HEADS * HEAD_W
SGU_GROUPS = 4
SGU_CHUNK = 128
SGU_W = 512
S5_GC = 16
S5_GROUPS = 32
S5_N = 64
S5_W = 512
S5_STATE = S5_GROUPS * S5_N
S5_SLOTS = 8
S5_BLOCKS = 4
N_BRANCH = 3
Z_OFF = QKV_W
U_OFF = Z_OFF + 2 * SGU_W
G_OFF = U_OFF + S5_W
IN_W = G_OFF + N_BRANCH * D_MODEL
PEER_HEADS = 8
PEER_NKEYS = 128
PEER_EXPERTS = PEER_NKEYS * PEER_NKEYS
PEER_DQ = 256
PEER_DHALF = 128
PEER_TOPK = 16
ALPHA = (2 * DEPTH) ** 0.25
LN_EPS = 1e-5
NEG_BIG = -1e30

VMEM_LIMIT = 56 * 1024 * 1024


GELU_C1 = 0.7978845608028654
GELU_C3 = GELU_C1 * 0.044715


def _gelu(x):
    return 0.5 * x * (1.0 + jnp.tanh(x * (GELU_C1 + GELU_C3 * (x * x))))


def _sigmoid(x):
    return 1.0 / (1.0 + jnp.exp(-x))


def _layernorm(x, g, b):
    mu = jnp.mean(x, axis=-1, keepdims=True)
    xc = x - mu
    var = jnp.mean(xc * xc, axis=-1, keepdims=True)
    return xc * lax.rsqrt(var + LN_EPS) * g + b


def _const_spec(shape):
    nd = len(shape)
    return pl.BlockSpec(shape, lambda *_: (0,) * nd, pipeline_mode=pl.Buffered(1))


def _params(sem, flags=None):
    return pltpu.CompilerParams(dimension_semantics=sem, vmem_limit_bytes=VMEM_LIMIT, flags=flags)


def _inproj_kernel(x_ref, w_ref, bg_ref, lng_ref, lnb_ref, ws_ref, bs_ref,
                   qkv_ref, sgu_ref, us5_ref, gate_ref, *, tm):
    xb = x_ref[...].astype(BF16)

    h = jnp.dot(xb, w_ref[:, 0:QKV_W], preferred_element_type=F32)
    nq = ATT_HEADS * HEAD_W
    qkv_ref[:, 0:nq] = (h[:, 0:nq] * (ATT_DK ** -0.5)).astype(BF16)
    qkv_ref[:, nq:QKV_W] = h[:, nq:QKV_W].astype(BF16)

    z = _gelu(jnp.dot(xb, w_ref[:, Z_OFF:U_OFF], preferred_element_type=F32))
    u = z[:, 0:SGU_W]
    v = _layernorm(z[:, SGU_W:2 * SGU_W], lng_ref[...], lnb_ref[...]).astype(BF16)
    for ci in range(tm // SGU_CHUNK):
        r0 = ci * SGU_CHUNK
        for g in range(SGU_GROUPS):
            c0 = g * 128
            sv = jnp.dot(ws_ref[g], v[r0:r0 + SGU_CHUNK, c0:c0 + 128],
                         preferred_element_type=F32) + bs_ref[:, c0:c0 + 128]
            sgu_ref[r0:r0 + SGU_CHUNK, c0:c0 + 128] = (
                u[r0:r0 + SGU_CHUNK, c0:c0 + 128] * sv).astype(BF16)

    us5_ref[...] = jnp.dot(xb, w_ref[:, U_OFF:G_OFF], preferred_element_type=F32)

    gl = jnp.dot(xb, w_ref[:, G_OFF:IN_W], preferred_element_type=F32)
    gate_ref[...] = _sigmoid(gl + bg_ref[...])


def _in_proj(x, w_in, b_gate, ln_g, ln_b, w_s, b_s_full, *, tm=256):
    n = x.shape[0]
    row = lambda w: pl.BlockSpec((tm, w), lambda i: (i, 0))
    return pl.pallas_call(
        functools.partial(_inproj_kernel, tm=tm),
        grid=(n // tm,),
        in_specs=[row(D_MODEL), _const_spec((D_MODEL, IN_W)), _const_spec((1, N_BRANCH * D_MODEL)),
                  _const_spec((1, SGU_W)), _const_spec((1, SGU_W)),
                  _const_spec((SGU_GROUPS, SGU_CHUNK, SGU_CHUNK)), _const_spec((SGU_CHUNK, SGU_W))],
        out_specs=[row(QKV_W), row(SGU_W), row(S5_W), row(N_BRANCH * D_MODEL)],
        out_shape=[jax.ShapeDtypeStruct((n, QKV_W), BF16), jax.ShapeDtypeStruct((n, SGU_W), BF16),
                   jax.ShapeDtypeStruct((n, S5_W), F32),
                   jax.ShapeDtypeStruct((n, N_BRANCH * D_MODEL), F32)],
        compiler_params=_params(("parallel",)),
        name="in_proj",
    )(x, w_in, b_gate, ln_g, ln_b, w_s, b_s_full)


def _attn_kernel(sc_ref, lamp_ref, q_ref, k_ref, v_ref, g_ref, o_ref,
                 qs_ref, bias_ref, va_ref, sa_ref, sb_ref, pa_ref, pb_ref,
                 mxa_ref, mxb_ref, ala_ref, alb_ref, m_ref, acc_ref, *, tq, tk, seq):
    hd = pl.program_id(1)
    qi = pl.program_id(2)
    slope = sc_ref[hd]
    nk = seq // tk
    nd = tk // tq
    jd = (qi * tq) // tk
    rows = 2 * tq
    assert nk == 1 or nk % 2 == 0

    @pl.when(qi == 0)
    def _():
        ri = lax.broadcasted_iota(jnp.int32, (rows, tk), 0)
        ri = jnp.where(ri >= tq, ri - tq, ri)
        ci = lax.broadcasted_iota(jnp.int32, (rows, tk), 1)
        rel = (ri - ci).astype(F32)
        bias_ref[0] = slope * rel
        bias_ref[1] = -slope * rel
        for v in range(nd):
            bias_ref[2 + v] = slope * jnp.abs(rel + float(v * tq))
        va_ref[:, 0:HEAD_W] = v_ref[...]
        va_ref[:, HEAD_W:2 * HEAD_W] = jnp.ones((seq, HEAD_W), BF16)

    q = q_ref[...]
    lane = lax.broadcasted_iota(jnp.int32, q.shape, 1)
    zero = jnp.zeros_like(q)
    qs_ref[0:tq, :] = jnp.where(lane < ATT_DK, q, zero)
    qs_ref[tq:rows, :] = jnp.where(lane >= ATT_DK, q, zero)

    m_ref[...] = jnp.full(m_ref.shape, NEG_BIG, F32)
    acc_ref[...] = jnp.zeros(acc_ref.shape, F32)

    def tile_const(j):
        off = jnp.abs(qi * tq - j * tk).astype(F32)
        return jnp.where(j == jd, 0.0, -slope * off)

    def key_start(j):
        return j * tk if isinstance(j, int) else pl.multiple_of(j * tk, tk)

    def scores(j, s_ref, mx_ref):
        k0 = key_start(j)
        s = lax.dot_general(qs_ref[...], k_ref[pl.ds(k0, tk), :], (((1,), (1,)), ((), ())),
                            preferred_element_type=F32)
        typ = jnp.where(j < jd, 0, jnp.where(j > jd, 1, 2 + qi % nd))
        s_ref[...] = s - bias_ref[typ]
        mx_ref[...] = jnp.max(s_ref[...], axis=1, keepdims=True) + tile_const(j)

    def probs(j, s_ref, mx_ref, p_ref, al_ref):
        m_old = m_ref[...]
        m_new = jnp.maximum(m_old, mx_ref[...])
        al_ref[...] = jnp.exp(m_old - m_new)
        p_ref[...] = jnp.exp(s_ref[...] - (m_new - tile_const(j))).astype(BF16)
        m_ref[...] = m_new

    def accumulate(j, p_ref, al_ref):
        k0 = key_start(j)
        acc_ref[...] = al_ref[...] * acc_ref[...] + jnp.dot(p_ref[...], va_ref[pl.ds(k0, tk), :],
                                                            preferred_element_type=F32)

    sa, sb = (sa_ref, mxa_ref), (sb_ref, mxb_ref)
    pa, pb = (pa_ref, ala_ref), (pb_ref, alb_ref)
    scores(0, *sa)
    if nk > 1:
        probs(0, *sa, *pa)
        scores(1, *sb)

        def pair(i, carry):
            j = 2 * i + 1
            accumulate(j - 1, *pa)
            probs(j, *sb, *pb)
            scores(j + 1, *sa)
            accumulate(j, *pb)
            probs(j + 1, *sa, *pa)
            scores(j + 2, *sb)
            return carry
        lax.fori_loop(0, nk // 2 - 1, pair, 0)
        accumulate(nk - 2, *pa)
        probs(nk - 1, *sb, *pb)
        accumulate(nk - 1, *pb)
    else:
        probs(0, *sa, *pa)
        accumulate(0, *pa)

    lp = lamp_ref[...]
    lam = (jnp.exp(jnp.sum(lp[0:1] * lp[1:2], axis=1, keepdims=True))
           - jnp.exp(jnp.sum(lp[2:3] * lp[3:4], axis=1, keepdims=True)) + sc_ref[ATT_HEADS])
    num = acc_ref[:, 0:HEAD_W]
    den = acc_ref[:, HEAD_W:2 * HEAD_W]
    o = num[0:tq] / den[0:tq] - lam * (num[tq:rows] / den[tq:rows])
    ms = jnp.mean(o * o, axis=-1, keepdims=True)
    o_ref[...] = (o * lax.rsqrt(ms + LN_EPS) * g_ref[...] * sc_ref[ATT_HEADS + 1]).astype(o_ref.dtype)


def _attention(scal, lam_params, qkv, norm_g, *, row_off, batch, seq, n_rows, tq=256, tk=1024):
    tk = min(tk, seq)
    qb0 = row_off // tq
    kb0 = row_off // seq
    nqb = seq // tq
    return pl.pallas_call(
        functools.partial(_attn_kernel, tq=tq, tk=tk, seq=seq),
        grid=(batch, ATT_HEADS, nqb),
        in_specs=[
            pl.BlockSpec(memory_space=pltpu.SMEM),
            pl.BlockSpec((4, ATT_DK), lambda b, h, i: (0, 0)),
            pl.BlockSpec((tq, HEAD_W), lambda b, h, i: (qb0 + b * nqb + i, h)),
            pl.BlockSpec((seq, HEAD_W), lambda b, h, i: (kb0 + b, ATT_HEADS + h)),
            pl.BlockSpec((seq, HEAD_W), lambda b, h, i: (kb0 + b, 2 * ATT_HEADS + h)),
            pl.BlockSpec((1, HEAD_W), lambda b, h, i: (0, 0)),
        ],
        out_specs=pl.BlockSpec((tq, HEAD_W), lambda b, h, i: (b * nqb + i, h)),
        out_shape=jax.ShapeDtypeStruct((n_rows, ATT_HEADS * HEAD_W), BF16),
        scratch_shapes=[pltpu.VMEM((2 * tq, HEAD_W), BF16),
                        pltpu.VMEM((2 + tk // tq, 2 * tq, tk), F32),
                        pltpu.VMEM((seq, 2 * HEAD_W), BF16),
                        pltpu.VMEM((2 * tq, tk), F32), pltpu.VMEM((2 * tq, tk), F32),
                        pltpu.VMEM((2 * tq, tk), BF16), pltpu.VMEM((2 * tq, tk), BF16),
                        pltpu.VMEM((2 * tq, 1), F32), pltpu.VMEM((2 * tq, 1), F32),
                        pltpu.VMEM((2 * tq, 1), F32), pltpu.VMEM((2 * tq, 1), F32),
                        pltpu.VMEM((2 * tq, 1), F32),
                        pltpu.VMEM((2 * tq, 2 * HEAD_W), F32)],
        compiler_params=_params(("parallel", "parallel", "arbitrary")),
        name="diff_attention",
    )(scal, lam_params, qkv, qkv, qkv, norm_g)


def _s5_kernel(u_ref, fm_ref, wb_ref, wcr_ref, wci_ref, ar_ref, ai_ref, y_ref,
               br_ref, bi_ref, xr_ref, xi_ref, *, tc):
    half = S5_STATE // S5_BLOCKS

    @pl.when(pl.program_id(0) == 0)
    def _():
        xr_ref[...] = jnp.zeros(xr_ref.shape, F32)
        xi_ref[...] = jnp.zeros(xi_ref.shape, F32)

    fm = fm_ref[...]
    for j in range(S5_BLOCKS):
        ub = u_ref[:, 128 * j:128 * (j + 1)]
        uf = ub * fm
        bu = jnp.dot(jnp.concatenate([uf, ub - uf], axis=1), wb_ref[j],
                     preferred_element_type=F32)
        br_ref[:, half * j:half * (j + 1)] = bu[:, 0:half]
        bi_ref[:, half * j:half * (j + 1)] = bu[:, half:2 * half]

    def body(t, carry):
        xr, xi = carry
        r0 = pl.multiple_of(t * S5_SLOTS, S5_SLOTS)
        ar = ar_ref[...]
        ai = ai_ref[...]
        nxr = ar * xr - ai * xi + br_ref[pl.ds(r0, S5_SLOTS), :]
        nxi = ar * xi + ai * xr + bi_ref[pl.ds(r0, S5_SLOTS), :]
        br_ref[pl.ds(r0, S5_SLOTS), :] = nxr
        bi_ref[pl.ds(r0, S5_SLOTS), :] = nxi
        return nxr, nxi

    xr, xi = lax.fori_loop(0, tc, body, (xr_ref[...], xi_ref[...]), unroll=4)
    xr_ref[...] = xr
    xi_ref[...] = xi

    is_fwd = fm.astype(F32) > 0.5
    for j in range(S5_BLOCKS):
        sr = br_ref[:, half * j:half * (j + 1)].astype(BF16)
        si = bi_ref[:, half * j:half * (j + 1)].astype(BF16)
        y = (jnp.dot(sr, wcr_ref[j], preferred_element_type=F32)
             + jnp.dot(si, wci_ref[j], preferred_element_type=F32))
        y_ref[:, 128 * j:128 * (j + 1)] = jnp.where(is_fwd, y[:, 0:128], y[:, 128:256])


def _s5_scan(u8, fmask, wb, wcr, wci, ar8, ai8, *, seq, tc=128):
    rows = tc * S5_SLOTS
    return pl.pallas_call(
        functools.partial(_s5_kernel, tc=tc),
        grid=(seq // tc,),
        in_specs=[pl.BlockSpec((rows, S5_W), lambda n: (n, 0)), _const_spec((rows, 128)),
                  _const_spec((S5_BLOCKS, 256, 1024)), _const_spec((S5_BLOCKS, 512, 256)),
                  _const_spec((S5_BLOCKS, 512, 256)),
                  _const_spec((S5_SLOTS, S5_STATE)), _const_spec((S5_SLOTS, S5_STATE))],
        out_specs=pl.BlockSpec((rows, S5_W), lambda n: (n, 0)),
        out_shape=jax.ShapeDtypeStruct((seq * S5_SLOTS, S5_W), F32),
        scratch_shapes=[pltpu.VMEM((rows, S5_STATE), F32), pltpu.VMEM((rows, S5_STATE), F32),
                        pltpu.VMEM((S5_SLOTS, S5_STATE), F32), pltpu.VMEM((S5_SLOTS, S5_STATE), F32)],
        compiler_params=_params(("arbitrary",)),
        name="s5_scan",
    )(u8, fmask, wb, wcr, wci, ar8, ai8)


def _s5_weights(a_re, a_im, log_step, b_re, b_im, c_re, c_im):
    dt = jnp.exp(log_step)[..., None]
    mag = jnp.exp(a_re * dt)
    abr = mag * jnp.cos(a_im * dt)
    abi = mag * jnp.sin(a_im * dt)
    den = a_re * a_re + a_im * a_im
    nr = abr - 1.0
    fr = (nr * a_re + abi * a_im) / den
    fi = (abi * a_re - nr * a_im) / den
    bbr = fr[..., None] * b_re - fi[..., None] * b_im
    bbi = fr[..., None] * b_im + fi[..., None] * b_re
    eye = jnp.eye(8, dtype=F32)

    def in_block(bb):
        bb = bb.reshape(2, S5_BLOCKS, 8, S5_N, S5_GC)
        return jnp.einsum('djgnc,gh->jdgchn', bb, eye).reshape(S5_BLOCKS, 256, 512)

    wb = jnp.concatenate([in_block(bbr), in_block(bbi)], axis=-1).astype(BF16)

    def out_block(cc):
        cc = cc.reshape(2, S5_BLOCKS, 8, S5_GC, S5_N)
        return jnp.einsum('djgcn,gh->jgndhc', cc, eye).reshape(S5_BLOCKS, 512, 256)

    wcr = out_block(c_re).astype(BF16)
    wci = out_block(-c_im).astype(BF16)
    return wb, wcr, wci, abr.reshape(2, S5_STATE), abi.reshape(2, S5_STATE)


def _s5_branch(us5, wb, wcr, wci, abr, abi, *, batch, seq):
    assert 2 * batch <= S5_SLOTS
    u = us5.astype(BF16).reshape(batch, seq, S5_W)
    pad = jnp.zeros((S5_SLOTS - 2 * batch, seq, S5_W), BF16)
    u8 = jnp.concatenate([u, u[:, ::-1], pad], axis=0)
    u8 = u8.transpose(1, 0, 2).reshape(seq * S5_SLOTS, S5_W)
    tc = 128
    fmask = jnp.broadcast_to((jnp.arange(S5_SLOTS) < batch).astype(BF16)[None, :, None],
                             (tc, S5_SLOTS, 128)).reshape(tc * S5_SLOTS, 128)
    slot_dir = jnp.array([0] * batch + [1] * batch, jnp.int32)
    live = jnp.concatenate([jnp.ones((2 * batch, 1), F32),
                            jnp.zeros((S5_SLOTS - 2 * batch, 1), F32)], axis=0)
    sel = jnp.concatenate([slot_dir, jnp.zeros((S5_SLOTS - 2 * batch,), jnp.int32)])
    ar8 = abr[sel] * live
    ai8 = abi[sel] * live
    y8 = _s5_scan(u8, fmask, wb, wcr, wci, ar8, ai8, seq=seq, tc=tc)
    y8 = y8.reshape(seq, S5_SLOTS, S5_W)
    yf = y8[:, 0:batch].transpose(1, 0, 2).reshape(batch * seq, S5_W)
    yb = y8[::-1, batch:2 * batch].transpose(1, 0, 2).reshape(batch * seq, S5_W)
    return yf, yb


def _merge_kernel(x_ref, att_ref, sgu_ref, us5_ref, yf_ref, yb_ref, gate_ref,
                  d_ref, gw_ref, gb_ref, wbr_ref, wo_ref, lg_ref, lb_ref, o_ref):
    y = _gelu(us5_ref[...] * d_ref[...] + yf_ref[...] + yb_ref[...])
    glu = _sigmoid(jnp.dot(y.astype(BF16), gw_ref[...], preferred_element_type=F32) + gb_ref[...])
    s5 = (y * glu).astype(BF16)
    branches = (att_ref[...], sgu_ref[...], s5)
    mix = None
    for i in range(N_BRANCH):
        proj = jnp.dot(branches[i], wbr_ref[i], preferred_element_type=F32)
        term = gate_ref[:, i * D_MODEL:(i + 1) * D_MODEL] * proj
        mix = term if mix is None else mix + term
    mixed = jnp.dot(mix.astype(BF16), wo_ref[...], preferred_element_type=F32)
    o_ref[...] = _layernorm(ALPHA * x_ref[...] + mixed, lg_ref[...], lb_ref[...])


def _merge(x, att, sgu, us5, yf, yb, gates, s5_d, glu_w, glu_b, w_branch, w_o, ln_g, ln_b, *, tm=512):
    n = x.shape[0]
    row = lambda w: pl.BlockSpec((tm, w), lambda i: (i, 0))
    return pl.pallas_call(
        _merge_kernel,
        grid=(n // tm,),
        in_specs=[row(D_MODEL), row(S5_W), row(S5_W), row(S5_W), row(S5_W), row(S5_W),
                  row(N_BRANCH * D_MODEL),
                  _const_spec((1, S5_W)), _const_spec((S5_W, S5_W)), _const_spec((1, S5_W)),
                  _const_spec((N_BRANCH, S5_W, D_MODEL)), _const_spec((D_MODEL, D_MODEL)),
                  _const_spec((1, D_MODEL)), _const_spec((1, D_MODEL))],
        out_specs=row(D_MODEL),
        out_shape=jax.ShapeDtypeStruct((n, D_MODEL), F32),
        compiler_params=_params(("parallel",)),
        name="merge",
    )(x, att, sgu, us5, yf, yb, gates, s5_d, glu_w, glu_b, w_branch, w_o, ln_g, ln_b)


def _top_rows(s, k, with_rank=False):
    row = lax.broadcasted_iota(jnp.int32, (k, s.shape[1]), 0)
    out = jnp.zeros((k, s.shape[1]), F32)
    rank = jnp.full(s.shape, float(k), F32)
    for i in range(k):
        m = jnp.max(s, axis=0, keepdims=True)
        out = jnp.where(row == i, m, out)
        hit = s == m
        if with_rank:
            rank = jnp.where(hit, float(i), rank)
        s = jnp.where(hit, -jnp.inf, s)
    return (out, rank) if with_rank else out


def _pair_candidates(v0, v1):
    r8 = lax.broadcasted_iota(jnp.int32, (8, v0.shape[1]), 0)
    r16 = lax.broadcasted_iota(jnp.int32, v0.shape, 0)
    ninf = -jnp.inf
    lo = v0[0:8]
    return jnp.concatenate([
        v0[0:1] + v1,
        v0[1:2] + v1[0:8],
        jnp.where(r16 >= 2, v0 + v1[0:1], ninf),
        jnp.where(r8 >= 2, lo + v1[1:2], ninf),
        jnp.where((r8 >= 2) & (r8 <= 4), lo + v1[2:3], ninf),
        jnp.where((r8 >= 2) & (r8 <= 3), lo + v1[3:4], ninf),
        jnp.where(r8 == 2, lo + v1[4:5], ninf),
    ], axis=0)


PEER_LANES = 128
PEER_PACK = 16
PEER_RGROUP = 4


def _peer_kernel(x_ref, wq_ref, keys_ref, u_ref, vt_ref, lg_ref, lb_ref, o_ref,
                 xb_ref, cnt_ref, e0_ref, rank_ref, e1_ref, gate_ref, acc_ref, *, tm, rpc):
    c = pl.program_id(1)
    strips = [slice(i * PEER_LANES, (i + 1) * PEER_LANES) for i in range(tm // PEER_LANES)]

    def gen_gate(r0):
        shape = (PEER_PACK, PEER_LANES)
        for si, ls in enumerate(strips):
            sums = [[None] * (PEER_NKEYS // PEER_PACK) for _ in range(PEER_RGROUP)]
            for h in range(PEER_HEADS):
                rk = rank_ref[h, si]
                ex = e1_ref[h, si]
                for r in range(PEER_RGROUP):
                    e1 = c * rpc + r0 + r
                    n = jnp.broadcast_to(cnt_ref[h, si, pl.ds(e1, 1), :], shape).astype(BF16)
                    w = jnp.broadcast_to(e0_ref[h, si, pl.ds(e1, 1), :], shape).astype(BF16)
                    for g in range(PEER_NKEYS // PEER_PACK):
                        rows = slice(g * PEER_PACK, (g + 1) * PEER_PACK)
                        term = jnp.where(rk[rows] < n, ex[rows], jnp.zeros(shape, BF16)) * w
                        sums[r][g] = term if sums[r][g] is None else sums[r][g] + term
            for r in range(PEER_RGROUP):
                for g in range(PEER_NKEYS // PEER_PACK):
                    g0 = (r0 + r) * PEER_NKEYS + g * PEER_PACK
                    gate_ref[g0:g0 + PEER_PACK, ls] = sums[r][g]

    @pl.when(c == 0)
    def _():
        xb = x_ref[...].astype(BF16)
        xb_ref[...] = xb
        q = jnp.dot(xb, wq_ref[...], preferred_element_type=F32).astype(BF16)
        for h in range(PEER_HEADS):
            st = []
            for half in range(2):
                c0 = (2 * h + half) * PEER_DHALF
                st.append(lax.dot_general(keys_ref[h, half], q[:, c0:c0 + PEER_DHALF],
                                          (((1,), (1,)), ((), ())), preferred_element_type=F32))
            for si, ls in enumerate(strips):
                a = st[0][:, ls]
                b = st[1][:, ls]
                v0 = _top_rows(a, PEER_TOPK)
                v1, rank1 = _top_rows(b, PEER_TOPK, with_rank=True)
                cs = _top_rows(_pair_candidates(v0, v1), PEER_TOPK)
                tau = cs[PEER_TOPK - 1:PEER_TOPK]
                z = jnp.sum(jnp.exp(cs - cs[0:1]), axis=0, keepdims=True)
                thr = tau - a
                cnt = jnp.zeros(a.shape, F32)
                for j in range(PEER_TOPK):
                    cnt = jnp.where(v1[j:j + 1] >= thr, float(j + 1), cnt)
                cnt_ref[h, si] = cnt
                e0_ref[h, si] = jnp.exp(a - v0[0:1]) * (0.5 / z)
                rank_ref[h, si] = rank1.astype(BF16)
                e1_ref[h, si] = jnp.exp(b - v1[0:1]).astype(BF16)
        acc_ref[...] = jnp.zeros(acc_ref.shape, F32)

    pe = PEER_RGROUP * PEER_NKEYS
    acts = []
    for r0 in range(0, rpc, PEER_RGROUP):
        p0 = r0 * PEER_NKEYS
        hid = lax.dot_general(u_ref[p0:p0 + pe, :], xb_ref[...], (((1,), (1,)), ((), ())),
                              preferred_element_type=F32)
        gen_gate(r0)
        t = jnp.tanh(hid * (GELU_C1 + GELU_C3 * (hid * hid)))
        acts.append(((gate_ref[p0:p0 + pe, :].astype(F32) * hid) * (1.0 + t)).astype(BF16))
    acc_ref[...] += jnp.dot(vt_ref[...], jnp.concatenate(acts, axis=0),
                            preferred_element_type=F32)

    @pl.when(c == pl.num_programs(1) - 1)
    def _():
        out = acc_ref[...].T
        o_ref[...] = _layernorm(ALPHA * x_ref[...] + out, lg_ref[...], lb_ref[...])


def _peer(x, w_q, keys, u_tab, v_tab_t, ln_g, ln_b, *, tm=512, rpc=16):
    n = x.shape[0]
    ec = rpc * PEER_NKEYS
    assert rpc % PEER_RGROUP == 0 and tm % PEER_LANES == 0
    route = (PEER_HEADS, tm // PEER_LANES, PEER_NKEYS, PEER_LANES)
    return pl.pallas_call(
        functools.partial(_peer_kernel, tm=tm, rpc=rpc),
        grid=(n // tm, PEER_EXPERTS // ec),
        in_specs=[pl.BlockSpec((tm, D_MODEL), lambda i, c: (i, 0)),
                  _const_spec((D_MODEL, PEER_HEADS * PEER_DQ)),
                  _const_spec((PEER_HEADS, 2, PEER_NKEYS, PEER_DHALF)),
                  pl.BlockSpec((ec, D_MODEL), lambda i, c: (c, 0)),
                  pl.BlockSpec((D_MODEL, ec), lambda i, c: (0, c)),
                  _const_spec((1, D_MODEL)), _const_spec((1, D_MODEL))],
        out_specs=pl.BlockSpec((tm, D_MODEL), lambda i, c: (i, 0)),
        out_shape=jax.ShapeDtypeStruct((n, D_MODEL), F32),
        scratch_shapes=[pltpu.VMEM((tm, D_MODEL), BF16),
                        pltpu.VMEM(route, F32), pltpu.VMEM(route, F32),
                        pltpu.VMEM(route, BF16), pltpu.VMEM(route, BF16),
                        pltpu.VMEM((ec, tm), BF16),
                        pltpu.VMEM((D_MODEL, tm), F32)],
        compiler_params=_params(("parallel", "arbitrary")),
        name="peer",
    )(x, w_q, keys, u_tab, v_tab_t, ln_g, ln_b)


def _layer(x, p, trunks):
    n = x.shape[0]
    qkv, sgu, us5, gates = _in_proj(x, p['w_in'], p['b_gate'], p['sgu_ln_g'], p['sgu_ln_b'],
                                    p['sgu_w_s'], p['sgu_b_s'])
    wb, wcr, wci, abr, abi = _s5_weights(p['s5_a_re'], p['s5_a_im'], p['s5_log_step'],
                                         p['s5_b_re'], p['s5_b_im'], p['s5_c_re'], p['s5_c_im'])
    att, yf, yb = [], [], []
    for row_off, batch, seq in trunks:
        nr = batch * seq
        att.append(_attention(p['att_scal'], p['lam_params'], qkv, p['att_norm_g'],
                              row_off=row_off, batch=batch, seq=seq, n_rows=nr))
        f, b = _s5_branch(lax.slice_in_dim(us5, row_off, row_off + nr, axis=0),
                          wb, wcr, wci, abr, abi, batch=batch, seq=seq)
        yf.append(f)
        yb.append(b)
    att = jnp.concatenate(att, axis=0)
    yf = jnp.concatenate(yf, axis=0)
    yb = jnp.concatenate(yb, axis=0)
    x = _merge(x, att, sgu, us5, yf, yb, gates, p['s5_d'], p['s5_glu_w'], p['s5_glu_b'],
               p['w_branch'], p['w_o'], p['ln1_g'], p['ln1_b'])
    x = _peer(x, p['peer_w_q'], p['peer_keys'], p['peer_u'], p['peer_v_t'], p['ln2_g'], p['ln2_b'])
    return x


def _prepare(w_in, b_gate, lambda_q1, lambda_k1, lambda_q2, lambda_k2, att_norm_g,
             sgu_ln_g, sgu_ln_b, sgu_w_s, sgu_b_s,
             s5_a_re, s5_a_im, s5_log_step, s5_b_re, s5_b_im, s5_c_re, s5_c_im, s5_d, s5_glu_w, s5_glu_b,
             w_branch, w_o, ln1_g, ln1_b, peer_w_q, peer_keys, peer_u, peer_v, ln2_g, ln2_b):
    depth = w_in.shape[0]
    lam_init = 0.8 - 0.6 * jnp.exp(-0.3 * jnp.arange(depth, dtype=F32))
    slopes = jnp.exp2(-8.0 * jnp.arange(1, ATT_HEADS + 1, dtype=F32) / ATT_HEADS)
    att_scal = jnp.concatenate([jnp.broadcast_to(slopes, (depth, ATT_HEADS)),
                                lam_init[:, None], 1.0 - lam_init[:, None]], axis=1)
    row = lambda a: a.reshape(depth, 1, -1).astype(F32)
    return dict(
        w_in=w_in.astype(BF16), b_gate=row(b_gate),
        att_scal=att_scal,
        lam_params=jnp.stack([lambda_q1, lambda_k1, lambda_q2, lambda_k2], axis=1).astype(F32),
        att_norm_g=row(att_norm_g),
        sgu_ln_g=row(sgu_ln_g), sgu_ln_b=row(sgu_ln_b), sgu_w_s=sgu_w_s.astype(BF16),
        sgu_b_s=jnp.repeat(jnp.swapaxes(sgu_b_s, 1, 2), SGU_W // SGU_GROUPS, axis=2).astype(F32),
        s5_a_re=s5_a_re, s5_a_im=s5_a_im, s5_log_step=s5_log_step, s5_b_re=s5_b_re, s5_b_im=s5_b_im,
        s5_c_re=s5_c_re, s5_c_im=s5_c_im,
        s5_d=row(s5_d), s5_glu_w=s5_glu_w.astype(BF16), s5_glu_b=row(s5_glu_b),
        w_branch=w_branch.astype(BF16), w_o=w_o.astype(BF16), ln1_g=row(ln1_g), ln1_b=row(ln1_b),
        peer_w_q=peer_w_q.astype(BF16), peer_keys=peer_keys.astype(BF16),
        peer_u=peer_u.astype(BF16), peer_v_t=jnp.swapaxes(peer_v.astype(BF16), 1, 2),
        ln2_g=row(ln2_g), ln2_b=row(ln2_b),
    )


def _trunk(xs, weights):
    trunks, off = [], 0
    for x in xs:
        b, s, _ = x.shape
        trunks.append((off, b, s))
        off += b * s
    x = jnp.concatenate([x.reshape(-1, D_MODEL) for x in xs], axis=0)
    params = _prepare(*weights)
    trunks = tuple(trunks)
    x, _ = lax.scan(lambda carry, p: (_layer(carry, p, trunks), None), x, params)
    outs = []
    for (off, b, s), xi in zip(trunks, xs):
        outs.append(x[off:off + b * s].reshape(xi.shape))
    return tuple(outs)


def kernel(x_prompt, x_sample, w_in, b_gate, lambda_q1, lambda_k1, lambda_q2, lambda_k2, att_norm_g, sgu_ln_g, sgu_ln_b, sgu_w_s, sgu_b_s, s5_a_re, s5_a_im, s5_log_step, s5_b_re, s5_b_im, s5_c_re, s5_c_im, s5_d, s5_glu_w, s5_glu_b, w_branch, w_o, ln1_g, ln1_b, peer_w_q, peer_keys, peer_u, peer_v, ln2_g, ln2_b):
    weights = (w_in, b_gate, lambda_q1, lambda_k1, lambda_q2, lambda_k2, att_norm_g,
               sgu_ln_g, sgu_ln_b, sgu_w_s, sgu_b_s,
               s5_a_re, s5_a_im, s5_log_step, s5_b_re, s5_b_im, s5_c_re, s5_c_im, s5_d, s5_glu_w, s5_glu_b,
               w_branch, w_o, ln1_g, ln1_b, peer_w_q, peer_keys, peer_u, peer_v, ln2_g, ln2_b)
    y_prompt, y_sample = _trunk([x_prompt, x_sample], weights)
    return (y_prompt, y_sample)
```

```python
import functools
import math

import jax
import jax.numpy as jnp
from jax import lax
from jax.experimental import pallas as pl
from jax.experimental.pallas import tpu as pltpu

F32 = jnp.float32
BF16 = jnp.bfloat16

D_MODEL = 1024
DEPTH = 4
ATT_HEADS = 4
ATT_DK = 64
HEAD_W = 2 * ATT_DK
QKV_W = 3 * ATT_HEADS * HEAD_W
SGU_GROUPS = 4
SGU_CHUNK = 128
SGU_W = 512
S5_GC = 16
S5_GROUPS = 32
S5_N = 64
S5_W = 512
S5_STATE = S5_GROUPS * S5_N
S5_SLOTS = 8
S5_BLOCKS = 4
N_BRANCH = 3
Z_OFF = QKV_W
U_OFF = Z_OFF + 2 * SGU_W
G_OFF = U_OFF + S5_W
IN_W = G_OFF + N_BRANCH * D_MODEL
PEER_HEADS = 8
PEER_NKEYS = 128
PEER_EXPERTS = PEER_NKEYS * PEER_NKEYS
PEER_DQ = 256
PEER_DHALF = 128
PEER_TOPK = 16
ALPHA = (2 * DEPTH) ** 0.25
LN_EPS = 1e-5
NEG_BIG = -1e30

VMEM_LIMIT = 56 * 1024 * 1024


GELU_C1 = 0.7978845608028654
GELU_C3 = GELU_C1 * 0.044715


def _gelu(x):
    return 0.5 * x * (1.0 + jnp.tanh(x * (GELU_C1 + GELU_C3 * (x * x))))


def _sigmoid(x):
    return 1.0 / (1.0 + jnp.exp(-x))


def _layernorm(x, g, b):
    mu = jnp.mean(x, axis=-1, keepdims=True)
    xc = x - mu
    var = jnp.mean(xc * xc, axis=-1, keepdims=True)
    return xc * lax.rsqrt(var + LN_EPS) * g + b


def _const_spec(shape):
    nd = len(shape)
    return pl.BlockSpec(shape, lambda *_: (0,) * nd, pipeline_mode=pl.Buffered(1))


def _params(sem, flags=None):
    return pltpu.CompilerParams(dimension_semantics=sem, vmem_limit_bytes=VMEM_LIMIT, flags=flags)


def _inproj_kernel(x_ref, w_ref, bg_ref, lng_ref, lnb_ref, ws_ref, bs_ref,
                   qkv_ref, sgu_ref, us5_ref, gate_ref, *, tm):
    xb = x_ref[...].astype(BF16)

    h = jnp.dot(xb, w_ref[:, 0:QKV_W], preferred_element_type=F32)
    nq = ATT_HEADS * HEAD_W
    qkv_ref[:, 0:nq] = (h[:, 0:nq] * (ATT_DK ** -0.5)).astype(BF16)
    qkv_ref[:, nq:QKV_W] = h[:, nq:QKV_W].astype(BF16)

    z = _gelu(jnp.dot(xb, w_ref[:, Z_OFF:U_OFF], preferred_element_type=F32))
    u = z[:, 0:SGU_W]
    v = _layernorm(z[:, SGU_W:2 * SGU_W], lng_ref[...], lnb_ref[...]).astype(BF16)
    for ci in range(tm // SGU_CHUNK):
        r0 = ci * SGU_CHUNK
        for g in range(SGU_GROUPS):
            c0 = g * 128
            sv = jnp.dot(ws_ref[g], v[r0:r0 + SGU_CHUNK, c0:c0 + 128],
                         preferred_element_type=F32) + bs_ref[:, c0:c0 + 128]
            sgu_ref[r0:r0 + SGU_CHUNK, c0:c0 + 128] = (
                u[r0:r0 + SGU_CHUNK, c0:c0 + 128] * sv).astype(BF16)

    us5_ref[...] = jnp.dot(xb, w_ref[:, U_OFF:G_OFF], preferred_element_type=F32)

    gl = jnp.dot(xb, w_ref[:, G_OFF:IN_W], preferred_element_type=F32)
    gate_ref[...] = _sigmoid(gl + bg_ref[...])


def _in_proj(x, w_in, b_gate, ln_g, ln_b, w_s, b_s_full, *, tm=256):
    n = x.shape[0]
    row = lambda w: pl.BlockSpec((tm, w), lambda i: (i, 0))
    return pl.pallas_call(
        functools.partial(_inproj_kernel, tm=tm),
        grid=(n // tm,),
        in_specs=[row(D_MODEL), _const_spec((D_MODEL, IN_W)), _const_spec((1, N_BRANCH * D_MODEL)),
                  _const_spec((1, SGU_W)), _const_spec((1, SGU_W)),
                  _const_spec((SGU_GROUPS, SGU_CHUNK, SGU_CHUNK)), _const_spec((SGU_CHUNK, SGU_W))],
        out_specs=[row(QKV_W), row(SGU_W), row(S5_W), row(N_BRANCH * D_MODEL)],
        out_shape=[jax.ShapeDtypeStruct((n, QKV_W), BF16), jax.ShapeDtypeStruct((n, SGU_W), BF16),
                   jax.ShapeDtypeStruct((n, S5_W), F32),
                   jax.ShapeDtypeStruct((n, N_BRANCH * D_MODEL), F32)],
        compiler_params=_params(("parallel",)),
        name="in_proj",
    )(x, w_in, b_gate, ln_g, ln_b, w_s, b_s_full)


def _attn_kernel(sc_ref, lamp_ref, q_ref, k_ref, v_ref, g_ref, o_ref,
                 qs_ref, bias_ref, va_ref, sa_ref, sb_ref, pa_ref, pb_ref,
                 mxa_ref, mxb_ref, ala_ref, alb_ref, m_ref, acc_ref, *, tq, tk, seq):
    hd = pl.program_id(1)
    qi = pl.program_id(2)
    slope = sc_ref[hd]
    nk = seq // tk
    nd = tk // tq
    jd = (qi * tq) // tk
    rows = 2 * tq
    assert nk == 1 or nk % 2 == 0

    @pl.when(qi == 0)
    def _():
        ri = lax.broadcasted_iota(jnp.int32, (rows, tk), 0)
        ri = jnp.where(ri >= tq, ri - tq, ri)
        ci = lax.broadcasted_iota(jnp.int32, (rows, tk), 1)
        rel = (ri - ci).astype(F32)
        bias_ref[0] = slope * rel
        bias_ref[1] = -slope * rel
        for v in range(nd):
            bias_ref[2 + v] = slope * jnp.abs(rel + float(v * tq))
        va_ref[:, 0:HEAD_W] = v_ref[...]
        va_ref[:, HEAD_W:2 * HEAD_W] = jnp.ones((seq, HEAD_W), BF16)

    q = q_ref[...]
    lane = lax.broadcasted_iota(jnp.int32, q.shape, 1)
    zero = jnp.zeros_like(q)
    qs_ref[0:tq, :] = jnp.where(lane < ATT_DK, q, zero)
    qs_ref[tq:rows, :] = jnp.where(lane >= ATT_DK, q, zero)

    m_ref[...] = jnp.full(m_ref.shape, NEG_BIG, F32)
    acc_ref[...] = jnp.zeros(acc_ref.shape, F32)

    def tile_const(j):
        off = jnp.abs(qi * tq - j * tk).astype(F32)
        return jnp.where(j == jd, 0.0, -slope * off)

    def key_start(j):
        return j * tk if isinstance(j, int) else pl.multiple_of(j * tk, tk)

    def scores(j, s_ref, mx_ref):
        k0 = key_start(j)
        s = lax.dot_general(qs_ref[...], k_ref[pl.ds(k0, tk), :], (((1,), (1,)), ((), ())),
                            preferred_element_type=F32)
        typ = jnp.where(j < jd, 0, jnp.where(j > jd, 1, 2 + qi % nd))
        s_ref[...] = s - bias_ref[typ]
        mx_ref[...] = jnp.max(s_ref[...], axis=1, keepdims=True) + tile_const(j)

    def probs(j, s_ref, mx_ref, p_ref, al_ref):
        m_old = m_ref[...]
        m_new = jnp.maximum(m_old, mx_ref[...])
        al_ref[...] = jnp.exp(m_old - m_new)
        p_ref[...] = jnp.exp(s_ref[...] - (m_new - tile_const(j))).astype(BF16)
        m_ref[...] = m_new

    def accumulate(j, p_ref, al_ref):
        k0 = key_start(j)
        acc_ref[...] = al_ref[...] * acc_ref[...] + jnp.dot(p_ref[...], va_ref[pl.ds(k0, tk), :],
                                                            preferred_element_type=F32)

    sa, sb = (sa_ref, mxa_ref), (sb_ref, mxb_ref)
    pa, pb = (pa_ref, ala_ref), (pb_ref, alb_ref)
    scores(0, *sa)
    if nk > 1:
        probs(0, *sa, *pa)
        scores(1, *sb)

        def pair(i, carry):
            j = 2 * i + 1
            accumulate(j - 1, *pa)
            probs(j, *sb, *pb)
            scores(j + 1, *sa)
            accumulate(j, *pb)
            probs(j + 1, *sa, *pa)
            scores(j + 2, *sb)
            return carry
        lax.fori_loop(0, nk // 2 - 1, pair, 0)
        accumulate(nk - 2, *pa)
        probs(nk - 1, *sb, *pb)
        accumulate(nk - 1, *pb)
    else:
        probs(0, *sa, *pa)
        accumulate(0, *pa)

    lp = lamp_ref[...]
    lam = (jnp.exp(jnp.sum(lp[0:1] * lp[1:2], axis=1, keepdims=True))
           - jnp.exp(jnp.sum(lp[2:3] * lp[3:4], axis=1, keepdims=True)) + sc_ref[ATT_HEADS])
    num = acc_ref[:, 0:HEAD_W]
    den = acc_ref[:, HEAD_W:2 * HEAD_W]
    o = num[0:tq] / den[0:tq] - lam * (num[tq:rows] / den[tq:rows])
    ms = jnp.mean(o * o, axis=-1, keepdims=True)
    o_ref[...] = (o * lax.rsqrt(ms + LN_EPS) * g_ref[...] * sc_ref[ATT_HEADS + 1]).astype(o_ref.dtype)


def _attention(scal, lam_params, qkv, norm_g, *, row_off, batch, seq, n_rows, tq=256, tk=1024):
    tk = min(tk, seq)
    qb0 = row_off // tq
    kb0 = row_off // seq
    nqb = seq // tq
    return pl.pallas_call(
        functools.partial(_attn_kernel, tq=tq, tk=tk, seq=seq),
        grid=(batch, ATT_HEADS, nqb),
        in_specs=[
            pl.BlockSpec(memory_space=pltpu.SMEM),
            pl.BlockSpec((4, ATT_DK), lambda b, h, i: (0, 0)),
            pl.BlockSpec((tq, HEAD_W), lambda b, h, i: (qb0 + b * nqb + i, h)),
            pl.BlockSpec((seq, HEAD_W), lambda b, h, i: (kb0 + b, ATT_HEADS + h)),
            pl.BlockSpec((seq, HEAD_W), lambda b, h, i: (kb0 + b, 2 * ATT_HEADS + h)),
            pl.BlockSpec((1, HEAD_W), lambda b, h, i: (0, 0)),
        ],
        out_specs=pl.BlockSpec((tq, HEAD_W), lambda b, h, i: (b * nqb + i, h)),
        out_shape=jax.ShapeDtypeStruct((n_rows, ATT_HEADS * HEAD_W), BF16),
        scratch_shapes=[pltpu.VMEM((2 * tq, HEAD_W), BF16),
                        pltpu.VMEM((2 + tk // tq, 2 * tq, tk), F32),
                        pltpu.VMEM((seq, 2 * HEAD_W), BF16),
                        pltpu.VMEM((2 * tq, tk), F32), pltpu.VMEM((2 * tq, tk), F32),
                        pltpu.VMEM((2 * tq, tk), BF16), pltpu.VMEM((2 * tq, tk), BF16),
                        pltpu.VMEM((2 * tq, 1), F32), pltpu.VMEM((2 * tq, 1), F32),
                        pltpu.VMEM((2 * tq, 1), F32), pltpu.VMEM((2 * tq, 1), F32),
                        pltpu.VMEM((2 * tq, 1), F32),
                        pltpu.VMEM((2 * tq, 2 * HEAD_W), F32)],
        compiler_params=_params(("parallel", "parallel", "arbitrary")),
        name="diff_attention",
    )(scal, lam_params, qkv, qkv, qkv, norm_g)


def _s5_kernel(u_ref, fm_ref, wb_ref, wcr_ref, wci_ref, ar_ref, ai_ref, y_ref,
               br_ref, bi_ref, xr_ref, xi_ref, *, tc):
    half = S5_STATE // S5_BLOCKS

    @pl.when(pl.program_id(0) == 0)
    def _():
        xr_ref[...] = jnp.zeros(xr_ref.shape, F32)
        xi_ref[...] = jnp.zeros(xi_ref.shape, F32)

    fm = fm_ref[...]
    for j in range(S5_BLOCKS):
        ub = u_ref[:, 128 * j:128 * (j + 1)]
        uf = ub * fm
        bu = jnp.dot(jnp.concatenate([uf, ub - uf], axis=1), wb_ref[j],
                     preferred_element_type=F32)
        br_ref[:, half * j:half * (j + 1)] = bu[:, 0:half]
        bi_ref[:, half * j:half * (j + 1)] = bu[:, half:2 * half]

    def body(t, carry):
        xr, xi = carry
        r0 = pl.multiple_of(t * S5_SLOTS, S5_SLOTS)
        ar = ar_ref[...]
        ai = ai_ref[...]
        nxr = ar * xr - ai * xi + br_ref[pl.ds(r0, S5_SLOTS), :]
        nxi = ar * xi + ai * xr + bi_ref[pl.ds(r0, S5_SLOTS), :]
        br_ref[pl.ds(r0, S5_SLOTS), :] = nxr
        bi_ref[pl.ds(r0, S5_SLOTS), :] = nxi
        return nxr, nxi

    xr, xi = lax.fori_loop(0, tc, body, (xr_ref[...], xi_ref[...]), unroll=4)
    xr_ref[...] = xr
    xi_ref[...] = xi

    is_fwd = fm.astype(F32) > 0.5
    for j in range(S5_BLOCKS):
        sr = br_ref[:, half * j:half * (j + 1)].astype(BF16)
        si = bi_ref[:, half * j:half * (j + 1)].astype(BF16)
        y = (jnp.dot(sr, wcr_ref[j], preferred_element_type=F32)
             + jnp.dot(si, wci_ref[j], preferred_element_type=F32))
        y_ref[:, 128 * j:128 * (j + 1)] = jnp.where(is_fwd, y[:, 0:128], y[:, 128:256])


def _s5_scan(u8, fmask, wb, wcr, wci, ar8, ai8, *, seq, tc=128):
    rows = tc * S5_SLOTS
    return pl.pallas_call(
        functools.partial(_s5_kernel, tc=tc),
        grid=(seq // tc,),
        in_specs=[pl.BlockSpec((rows, S5_W), lambda n: (n, 0)), _const_spec((rows, 128)),
                  _const_spec((S5_BLOCKS, 256, 1024)), _const_spec((S5_BLOCKS, 512, 256)),
                  _const_spec((S5_BLOCKS, 512, 256)),
                  _const_spec((S5_SLOTS, S5_STATE)), _const_spec((S5_SLOTS, S5_STATE))],
        out_specs=pl.BlockSpec((rows, S5_W), lambda n: (n, 0)),
        out_shape=jax.ShapeDtypeStruct((seq * S5_SLOTS, S5_W), F32),
        scratch_shapes=[pltpu.VMEM((rows, S5_STATE), F32), pltpu.VMEM((rows, S5_STATE), F32),
                        pltpu.VMEM((S5_SLOTS, S5_STATE), F32), pltpu.VMEM((S5_SLOTS, S5_STATE), F32)],
        compiler_params=_params(("arbitrary",)),
        name="s5_scan",
    )(u8, fmask, wb, wcr, wci, ar8, ai8)


def _s5_weights(a_re, a_im, log_step, b_re, b_im, c_re, c_im):
    dt = jnp.exp(log_step)[..., None]
    mag = jnp.exp(a_re * dt)
    abr = mag * jnp.cos(a_im * dt)
    abi = mag * jnp.sin(a_im * dt)
    den = a_re * a_re + a_im * a_im
    nr = abr - 1.0
    fr = (nr * a_re + abi * a_im) / den
    fi = (abi * a_re - nr * a_im) / den
    bbr = fr[..., None] * b_re - fi[..., None] * b_im
    bbi = fr[..., None] * b_im + fi[..., None] * b_re
    eye = jnp.eye(8, dtype=F32)

    def in_block(bb):
        bb = bb.reshape(2, S5_BLOCKS, 8, S5_N, S5_GC)
        return jnp.einsum('djgnc,gh->jdgchn', bb, eye).reshape(S5_BLOCKS, 256, 512)

    wb = jnp.concatenate([in_block(bbr), in_block(bbi)], axis=-1).astype(BF16)

    def out_block(cc):
        cc = cc.reshape(2, S5_BLOCKS, 8, S5_GC, S5_N)
        return jnp.einsum('djgcn,gh->jgndhc', cc, eye).reshape(S5_BLOCKS, 512, 256)

    wcr = out_block(c_re).astype(BF16)
    wci = out_block(-c_im).astype(BF16)
    return wb, wcr, wci, abr.reshape(2, S5_STATE), abi.reshape(2, S5_STATE)


def _s5_branch(us5, wb, wcr, wci, abr, abi, *, batch, seq):
    assert 2 * batch <= S5_SLOTS
    u = us5.astype(BF16).reshape(batch, seq, S5_W)
    pad = jnp.zeros((S5_SLOTS - 2 * batch, seq, S5_W), BF16)
    u8 = jnp.concatenate([u, u[:, ::-1], pad], axis=0)
    u8 = u8.transpose(1, 0, 2).reshape(seq * S5_SLOTS, S5_W)
    tc = 128
    fmask = jnp.broadcast_to((jnp.arange(S5_SLOTS) < batch).astype(BF16)[None, :, None],
                             (tc, S5_SLOTS, 128)).reshape(tc * S5_SLOTS, 128)
    slot_dir = jnp.array([0] * batch + [1] * batch, jnp.int32)
    live = jnp.concatenate([jnp.ones((2 * batch, 1), F32),
                            jnp.zeros((S5_SLOTS - 2 * batch, 1), F32)], axis=0)
    sel = jnp.concatenate([slot_dir, jnp.zeros((S5_SLOTS - 2 * batch,), jnp.int32)])
    ar8 = abr[sel] * live
    ai8 = abi[sel] * live
    y8 = _s5_scan(u8, fmask, wb, wcr, wci, ar8, ai8, seq=seq, tc=tc)
    y8 = y8.reshape(seq, S5_SLOTS, S5_W)
    yf = y8[:, 0:batch].transpose(1, 0, 2).reshape(batch * seq, S5_W)
    yb = jnp.flip(y8[:, batch:2 * batch].transpose(1, 0, 2), axis=1).reshape(batch * seq, S5_W)
    return yf, yb


def _merge_kernel(x_ref, att_ref, sgu_ref, us5_ref, yf_ref, yb_ref, gate_ref,
                  d_ref, gw_ref, gb_ref, wbr_ref, wo_ref, lg_ref, lb_ref, o_ref):
    y = _gelu(us5_ref[...] * d_ref[...] + yf_ref[...] + yb_ref[...])
    glu = _sigmoid(jnp.dot(y.astype(BF16), gw_ref[...], preferred_element_type=F32) + gb_ref[...])
    s5 = (y * glu).astype(BF16)
    branches = (att_ref[...], sgu_ref[...], s5)
    mix = None
    for i in range(N_BRANCH):
        proj = jnp.dot(branches[i], wbr_ref[i], preferred_element_type=F32)
        term = gate_ref[:, i * D_MODEL:(i + 1) * D_MODEL] * proj
        mix = term if mix is None else mix + term
    mixed = jnp.dot(mix.astype(BF16), wo_ref[...], preferred_element_type=F32)
    o_ref[...] = _layernorm(ALPHA * x_ref[...] + mixed, lg_ref[...], lb_ref[...])


def _merge(x, att, sgu, us5, yf, yb, gates, s5_d, glu_w, glu_b, w_branch, w_o, ln_g, ln_b, *, tm=512):
    n = x.shape[0]
    row = lambda w: pl.BlockSpec((tm, w), lambda i: (i, 0))
    return pl.pallas_call(
        _merge_kernel,
        grid=(n // tm,),
        in_specs=[row(D_MODEL), row(S5_W), row(S5_W), row(S5_W), row(S5_W), row(S5_W),
                  row(N_BRANCH * D_MODEL),
                  _const_spec((1, S5_W)), _const_spec((S5_W, S5_W)), _const_spec((1, S5_W)),
                  _const_spec((N_BRANCH, S5_W, D_MODEL)), _const_spec((D_MODEL, D_MODEL)),
                  _const_spec((1, D_MODEL)), _const_spec((1, D_MODEL))],
        out_specs=row(D_MODEL),
        out_shape=jax.ShapeDtypeStruct((n, D_MODEL), F32),
        compiler_params=_params(("parallel",)),
        name="merge",
    )(x, att, sgu, us5, yf, yb, gates, s5_d, glu_w, glu_b, w_branch, w_o, ln_g, ln_b)


def _top_rows(s, k, with_rank=False):
    row = lax.broadcasted_iota(jnp.int32, (k, s.shape[1]), 0)
    out = jnp.zeros((k, s.shape[1]), F32)
    rank = jnp.full(s.shape, float(k), F32)
    for i in range(k):
        m = jnp.max(s, axis=0, keepdims=True)
        out = jnp.where(row == i, m, out)
        hit = s == m
        if with_rank:
            rank = jnp.where(hit, float(i), rank)
        s = jnp.where(hit, -jnp.inf, s)
    return (out, rank) if with_rank else out


def _pair_candidates(v0, v1):
    r8 = lax.broadcasted_iota(jnp.int32, (8, v0.shape[1]), 0)
    r16 = lax.broadcasted_iota(jnp.int32, v0.shape, 0)
    ninf = -jnp.inf
    lo = v0[0:8]
    return jnp.concatenate([
        v0[0:1] + v1,
        v0[1:2] + v1[0:8],
        jnp.where(r16 >= 2, v0 + v1[0:1], ninf),
        jnp.where(r8 >= 2, lo + v1[1:2], ninf),
        jnp.where((r8 >= 2) & (r8 <= 4), lo + v1[2:3], ninf),
        jnp.where((r8 >= 2) & (r8 <= 3), lo + v1[3:4], ninf),
        jnp.where(r8 == 2, lo + v1[4:5], ninf),
    ], axis=0)


PEER_LANES = 128
PEER_PACK = 16
PEER_RGROUP = 4


def _peer_kernel(x_ref, wq_ref, keys_ref, u_ref, vt_ref, lg_ref, lb_ref, o_ref,
                 xb_ref, cnt_ref, e0_ref, rank_ref, e1_ref, gate_ref, acc_ref, *, tm, rpc):
    c = pl.program_id(1)
    strips = [slice(i * PEER_LANES, (i + 1) * PEER_LANES) for i in range(tm // PEER_LANES)]

    def gen_gate(r0):
        shape = (PEER_PACK, PEER_LANES)
        for si, ls in enumerate(strips):
            sums = [[None] * (PEER_NKEYS // PEER_PACK) for _ in range(PEER_RGROUP)]
            for h in range(PEER_HEADS):
                rk = rank_ref[h, si]
                ex = e1_ref[h, si]
                for r in range(PEER_RGROUP):
                    e1 = c * rpc + r0 + r
                    n = jnp.broadcast_to(cnt_ref[h, si, pl.ds(e1, 1), :], shape).astype(BF16)
                    w = jnp.broadcast_to(e0_ref[h, si, pl.ds(e1, 1), :], shape).astype(BF16)
                    for g in range(PEER_NKEYS // PEER_PACK):
                        rows = slice(g * PEER_PACK, (g + 1) * PEER_PACK)
                        term = jnp.where(rk[rows] < n, ex[rows], jnp.zeros(shape, BF16)) * w
                        sums[r][g] = term if sums[r][g] is None else sums[r][g] + term
            for r in range(PEER_RGROUP):
                for g in range(PEER_NKEYS // PEER_PACK):
                    g0 = (r0 + r) * PEER_NKEYS + g * PEER_PACK
                    gate_ref[g0:g0 + PEER_PACK, ls] = sums[r][g]

    @pl.when(c == 0)
    def _():
        xb = x_ref[...].astype(BF16)
        xb_ref[...] = xb
        q = jnp.dot(xb, wq_ref[...], preferred_element_type=F32).astype(BF16)
        for h in range(PEER_HEADS):
            st = []
            for half in range(2):
                c0 = (2 * h + half) * PEER_DHALF
                st.append(lax.dot_general(keys_ref[h, half], q[:, c0:c0 + PEER_DHALF],
                                          (((1,), (1,)), ((), ())), preferred_element_type=F32))
            for si, ls in enumerate(strips):
                a = st[0][:, ls]
                b = st[1][:, ls]
                v0 = _top_rows(a, PEER_TOPK)
                v1, rank1 = _top_rows(b, PEER_TOPK, with_rank=True)
                cs = _top_rows(_pair_candidates(v0, v1), PEER_TOPK)
                tau = cs[PEER_TOPK - 1:PEER_TOPK]
                z = jnp.sum(jnp.exp(cs - cs[0:1]), axis=0, keepdims=True)
                thr = tau - a
                cnt = jnp.zeros(a.shape, F32)
                for j in range(PEER_TOPK):
                    cnt = jnp.where(v1[j:j + 1] >= thr, float(j + 1), cnt)
                cnt_ref[h, si] = cnt
                e0_ref[h, si] = jnp.exp(a - v0[0:1]) * (0.5 / z)
                rank_ref[h, si] = rank1.astype(BF16)
                e1_ref[h, si] = jnp.exp(b - v1[0:1]).astype(BF16)
        acc_ref[...] = jnp.zeros(acc_ref.shape, F32)

    pe = PEER_RGROUP * PEER_NKEYS
    acts = []
    for r0 in range(0, rpc, PEER_RGROUP):
        p0 = r0 * PEER_NKEYS
        hid = lax.dot_general(u_ref[p0:p0 + pe, :], xb_ref[...], (((1,), (1,)), ((), ())),
                              preferred_element_type=F32)
        gen_gate(r0)
        t = jnp.tanh(hid * (GELU_C1 + GELU_C3 * (hid * hid)))
        acts.append(((gate_ref[p0:p0 + pe, :].astype(F32) * hid) * (1.0 + t)).astype(BF16))
    acc_ref[...] += jnp.dot(vt_ref[...], jnp.concatenate(acts, axis=0),
                            preferred_element_type=F32)

    @pl.when(c == pl.num_programs(1) - 1)
    def _():
        out = acc_ref[...].T
        o_ref[...] = _layernorm(ALPHA * x_ref[...] + out, lg_ref[...], lb_ref[...])


def _peer(x, w_q, keys, u_tab, v_tab_t, ln_g, ln_b, *, tm=512, rpc=16):
    n = x.shape[0]
    ec = rpc * PEER_NKEYS
    assert rpc % PEER_RGROUP == 0 and tm % PEER_LANES == 0
    route = (PEER_HEADS, tm // PEER_LANES, PEER_NKEYS, PEER_LANES)
    return pl.pallas_call(
        functools.partial(_peer_kernel, tm=tm, rpc=rpc),
        grid=(n // tm, PEER_EXPERTS // ec),
        in_specs=[pl.BlockSpec((tm, D_MODEL), lambda i, c: (i, 0)),
                  _const_spec((D_MODEL, PEER_HEADS * PEER_DQ)),
                  _const_spec((PEER_HEADS, 2, PEER_NKEYS, PEER_DHALF)),
                  pl.BlockSpec((ec, D_MODEL), lambda i, c: (c, 0)),
                  pl.BlockSpec((D_MODEL, ec), lambda i, c: (0, c)),
                  _const_spec((1, D_MODEL)), _const_spec((1, D_MODEL))],
        out_specs=pl.BlockSpec((tm, D_MODEL), lambda i, c: (i, 0)),
        out_shape=jax.ShapeDtypeStruct((n, D_MODEL), F32),
        scratch_shapes=[pltpu.VMEM((tm, D_MODEL), BF16),
                        pltpu.VMEM(route, F32), pltpu.VMEM(route, F32),
                        pltpu.VMEM(route, BF16), pltpu.VMEM(route, BF16),
                        pltpu.VMEM((ec, tm), BF16),
                        pltpu.VMEM((D_MODEL, tm), F32)],
        compiler_params=_params(("parallel", "arbitrary")),
        name="peer",
    )(x, w_q, keys, u_tab, v_tab_t, ln_g, ln_b)


def _layer(x, p, s5w, *, batch, seq):
    qkv, sgu, us5, gates = _in_proj(x, p['w_in'], p['b_gate'], p['sgu_ln_g'], p['sgu_ln_b'],
                                    p['sgu_w_s'], p['sgu_b_s'])
    att = _attention(p['att_scal'], p['lam_params'], qkv, p['att_norm_g'],
                     row_off=0, batch=batch, seq=seq, n_rows=batch * seq)
    yf, yb = _s5_branch(us5, *s5w, batch=batch, seq=seq)
    x = _merge(x, att, sgu, us5, yf, yb, gates, p['s5_d'], p['s5_glu_w'], p['s5_glu_b'],
               p['w_branch'], p['w_o'], p['ln1_g'], p['ln1_b'])
    return _peer(x, p['peer_w_q'], p['peer_keys'], p['peer_u'], p['peer_v_t'], p['ln2_g'], p['ln2_b'])


def _prepare(w_in, b_gate, lambda_q1, lambda_k1, lambda_q2, lambda_k2, att_norm_g,
             sgu_ln_g, sgu_ln_b, sgu_w_s, sgu_b_s,
             s5_a_re, s5_a_im, s5_log_step, s5_b_re, s5_b_im, s5_c_re, s5_c_im, s5_d, s5_glu_w, s5_glu_b,
             w_branch, w_o, ln1_g, ln1_b, peer_w_q, peer_keys, peer_u, peer_v, ln2_g, ln2_b):
    depth = w_in.shape[0]
    lam_init = 0.8 - 0.6 * jnp.exp(-0.3 * jnp.arange(depth, dtype=F32))
    slopes = jnp.exp2(-8.0 * jnp.arange(1, ATT_HEADS + 1, dtype=F32) / ATT_HEADS)
    att_scal = jnp.concatenate([jnp.broadcast_to(slopes, (depth, ATT_HEADS)),
                                lam_init[:, None], 1.0 - lam_init[:, None]], axis=1)
    row = lambda a: a.reshape(depth, 1, -1).astype(F32)
    return dict(
        w_in=w_in.astype(BF16), b_gate=row(b_gate),
        att_scal=att_scal,
        lam_params=jnp.stack([lambda_q1, lambda_k1, lambda_q2, lambda_k2], axis=1).astype(F32),
        att_norm_g=row(att_norm_g),
        sgu_ln_g=row(sgu_ln_g), sgu_ln_b=row(sgu_ln_b), sgu_w_s=sgu_w_s.astype(BF16),
        sgu_b_s=jnp.repeat(jnp.swapaxes(sgu_b_s, 1, 2), SGU_W // SGU_GROUPS, axis=2).astype(F32),
        s5_a_re=s5_a_re, s5_a_im=s5_a_im, s5_log_step=s5_log_step, s5_b_re=s5_b_re, s5_b_im=s5_b_im,
        s5_c_re=s5_c_re, s5_c_im=s5_c_im,
        s5_d=row(s5_d), s5_glu_w=s5_glu_w.astype(BF16), s5_glu_b=row(s5_glu_b),
        w_branch=w_branch.astype(BF16), w_o=w_o.astype(BF16), ln1_g=row(ln1_g), ln1_b=row(ln1_b),
        peer_w_q=peer_w_q.astype(BF16), peer_keys=peer_keys.astype(BF16),
        peer_u=peer_u.astype(BF16), peer_v_t=jnp.swapaxes(peer_v.astype(BF16), 1, 2),
        ln2_g=row(ln2_g), ln2_b=row(ln2_b),
    )


def _trunk(xs, weights):
    shapes = [x.shape for x in xs]
    params = _prepare(*weights)

    def step(carry, p):
        s5w = _s5_weights(p['s5_a_re'], p['s5_a_im'], p['s5_log_step'],
                          p['s5_b_re'], p['s5_b_im'], p['s5_c_re'], p['s5_c_im'])
        return tuple(_layer(x, p, s5w, batch=b, seq=s)
                     for x, (b, s, _) in zip(carry, shapes)), None

    flat = tuple(x.reshape(-1, D_MODEL) for x in xs)
    flat, _ = lax.scan(step, flat, params)
    return tuple(x.reshape(shape) for x, shape in zip(flat, shapes))


def kernel(x_prompt, x_sample, w_in, b_gate, lambda_q1, lambda_k1, lambda_q2, lambda_k2, att_norm_g, sgu_ln_g, sgu_ln_b, sgu_w_s, sgu_b_s, s5_a_re, s5_a_im, s5_log_step, s5_b_re, s5_b_im, s5_c_re, s5_c_im, s5_d, s5_glu_w, s5_glu_b, w_branch, w_o, ln1_g, ln1_b, peer_w_q, peer_keys, peer_u, peer_v, ln2_g, ln2_b):
    weights = (w_in, b_gate, lambda_q1, lambda_k1, lambda_q2, lambda_k2, att_norm_g,
               sgu_ln_g, sgu_ln_b, sgu_w_s, sgu_b_s,
               s5_a_re, s5_a_im, s5_log_step, s5_b_re, s5_b_im, s5_c_re, s5_c_im, s5_d, s5_glu_w, s5_glu_b,
               w_branch, w_o, ln1_g, ln1_b, peer_w_q, peer_keys, peer_u, peer_v, ln2_g, ln2_b)
    y_prompt, y_sample = _trunk([x_prompt, x_sample], weights)
    return (y_prompt, y_sample)
```

```python
import functools
import math

import jax
import jax.numpy as jnp
from jax import lax
from jax.experimental import pallas as pl
from jax.experimental.pallas import tpu as pltpu

F32 = jnp.float32
BF16 = jnp.bfloat16

D_MODEL = 1024
DEPTH = 4
ATT_HEADS = 4
ATT_DK = 64
HEAD_W = 2 * ATT_DK
QKV_W = 3 * ATT_HEADS * HEAD_W
SGU_GROUPS = 4
SGU_CHUNK = 128
SGU_W = 512
S5_GC = 16
S5_GROUPS = 32
S5_N = 64
S5_W = 512
S5_STATE = S5_GROUPS * S5_N
S5_SLOTS = 8
S5_BLOCKS = 4
N_BRANCH = 3
Z_OFF = QKV_W
U_OFF = Z_OFF + 2 * SGU_W
G_OFF = U_OFF + S5_W
IN_W = G_OFF + N_BRANCH * D_MODEL
PEER_HEADS = 8
PEER_NKEYS = 128
PEER_EXPERTS = PEER_NKEYS * PEER_NKEYS
PEER_DQ = 256
PEER_DHALF = 128
PEER_TOPK = 16
ALPHA = (2 * DEPTH) ** 0.25
LN_EPS = 1e-5
NEG_BIG = -1e30

VMEM_LIMIT = 56 * 1024 * 1024


GELU_C1 = 0.7978845608028654
GELU_C3 = GELU_C1 * 0.044715


def _gelu(x):
    return 0.5 * x * (1.0 + jnp.tanh(x * (GELU_C1 + GELU_C3 * (x * x))))


def _sigmoid(x):
    return 1.0 / (1.0 + jnp.exp(-x))


def _layernorm(x, g, b):
    mu = jnp.mean(x, axis=-1, keepdims=True)
    xc = x - mu
    var = jnp.mean(xc * xc, axis=-1, keepdims=True)
    return xc * lax.rsqrt(var + LN_EPS) * g + b


def _const_spec(shape):
    nd = len(shape)
    return pl.BlockSpec(shape, lambda *_: (0,) * nd, pipeline_mode=pl.Buffered(1))


def _params(sem, flags=None):
    return pltpu.CompilerParams(dimension_semantics=sem, vmem_limit_bytes=VMEM_LIMIT, flags=flags)


def _inproj_kernel(x_ref, w_ref, bg_ref, lng_ref, lnb_ref, ws_ref, bs_ref,
                   qkv_ref, sgu_ref, us5_ref, gate_ref, *, tm):
    xb = x_ref[...].astype(BF16)

    h = jnp.dot(xb, w_ref[:, 0:QKV_W], preferred_element_type=F32)
    nq = ATT_HEADS * HEAD_W
    qkv_ref[:, 0:nq] = (h[:, 0:nq] * (ATT_DK ** -0.5)).astype(BF16)
    qkv_ref[:, nq:QKV_W] = h[:, nq:QKV_W].astype(BF16)

    z = _gelu(jnp.dot(xb, w_ref[:, Z_OFF:U_OFF], preferred_element_type=F32))
    u = z[:, 0:SGU_W]
    v = _layernorm(z[:, SGU_W:2 * SGU_W], lng_ref[...], lnb_ref[...]).astype(BF16)
    for ci in range(tm // SGU_CHUNK):
        r0 = ci * SGU_CHUNK
        for g in range(SGU_GROUPS):
            c0 = g * 128
            sv = jnp.dot(ws_ref[g], v[r0:r0 + SGU_CHUNK, c0:c0 + 128],
                         preferred_element_type=F32) + bs_ref[:, c0:c0 + 128]
            sgu_ref[r0:r0 + SGU_CHUNK, c0:c0 + 128] = (
                u[r0:r0 + SGU_CHUNK, c0:c0 + 128] * sv).astype(BF16)

    us5_ref[...] = jnp.dot(xb, w_ref[:, U_OFF:G_OFF], preferred_element_type=F32)

    gl = jnp.dot(xb, w_ref[:, G_OFF:IN_W], preferred_element_type=F32)
    gate_ref[...] = _sigmoid(gl + bg_ref[...])


def _in_proj(x, w_in, b_gate, ln_g, ln_b, w_s, b_s_full, *, tm=256):
    n = x.shape[0]
    row = lambda w: pl.BlockSpec((tm, w), lambda i: (i, 0))
    return pl.pallas_call(
        functools.partial(_inproj_kernel, tm=tm),
        grid=(n // tm,),
        in_specs=[row(D_MODEL), _const_spec((D_MODEL, IN_W)), _const_spec((1, N_BRANCH * D_MODEL)),
                  _const_spec((1, SGU_W)), _const_spec((1, SGU_W)),
                  _const_spec((SGU_GROUPS, SGU_CHUNK, SGU_CHUNK)), _const_spec((SGU_CHUNK, SGU_W))],
        out_specs=[row(QKV_W), row(SGU_W), row(S5_W), row(N_BRANCH * D_MODEL)],
        out_shape=[jax.ShapeDtypeStruct((n, QKV_W), BF16), jax.ShapeDtypeStruct((n, SGU_W), BF16),
                   jax.ShapeDtypeStruct((n, S5_W), F32),
                   jax.ShapeDtypeStruct((n, N_BRANCH * D_MODEL), F32)],
        compiler_params=_params(("parallel",)),
        name="in_proj",
    )(x, w_in, b_gate, ln_g, ln_b, w_s, b_s_full)


def _attn_kernel(sc_ref, lamp_ref, q_ref, k_ref, v_ref, g_ref, o_ref,
                 qs_ref, bias_ref, va_ref, sa_ref, sb_ref, pa_ref, pb_ref,
                 mxa_ref, mxb_ref, ala_ref, alb_ref, m_ref, acc_ref, *, tq, tk, seq):
    hd = pl.program_id(1)
    qi = pl.program_id(2)
    slope = sc_ref[hd]
    nk = seq // tk
    nd = tk // tq
    jd = (qi * tq) // tk
    rows = 2 * tq
    assert nk == 1 or nk % 2 == 0

    @pl.when(qi == 0)
    def _():
        ri = lax.broadcasted_iota(jnp.int32, (rows, tk), 0)
        ri = jnp.where(ri >= tq, ri - tq, ri)
        ci = lax.broadcasted_iota(jnp.int32, (rows, tk), 1)
        rel = (ri - ci).astype(F32)
        bias_ref[0] = slope * rel
        bias_ref[1] = -slope * rel
        for v in range(nd):
            bias_ref[2 + v] = slope * jnp.abs(rel + float(v * tq))
        va_ref[:, 0:HEAD_W] = v_ref[...]
        va_ref[:, HEAD_W:2 * HEAD_W] = jnp.ones((seq, HEAD_W), BF16)

    q = q_ref[...]
    lane = lax.broadcasted_iota(jnp.int32, q.shape, 1)
    zero = jnp.zeros_like(q)
    qs_ref[0:tq, :] = jnp.where(lane < ATT_DK, q, zero)
    qs_ref[tq:rows, :] = jnp.where(lane >= ATT_DK, q, zero)

    m_ref[...] = jnp.full(m_ref.shape, NEG_BIG, F32)
    acc_ref[...] = jnp.zeros(acc_ref.shape, F32)

    def tile_const(j):
        off = jnp.abs(qi * tq - j * tk).astype(F32)
        return jnp.where(j == jd, 0.0, -slope * off)

    def key_start(j):
        return j * tk if isinstance(j, int) else pl.multiple_of(j * tk, tk)

    def scores(j, s_ref, mx_ref):
        k0 = key_start(j)
        s = lax.dot_general(qs_ref[...], k_ref[pl.ds(k0, tk), :], (((1,), (1,)), ((), ())),
                            preferred_element_type=F32)
        typ = jnp.where(j < jd, 0, jnp.where(j > jd, 1, 2 + qi % nd))
        s_ref[...] = s - bias_ref[typ]
        mx_ref[...] = jnp.max(s_ref[...], axis=1, keepdims=True) + tile_const(j)

    def probs(j, s_ref, mx_ref, p_ref, al_ref):
        m_old = m_ref[...]
        m_new = jnp.maximum(m_old, mx_ref[...])
        al_ref[...] = jnp.exp(m_old - m_new)
        p_ref[...] = jnp.exp(s_ref[...] - (m_new - tile_const(j))).astype(BF16)
        m_ref[...] = m_new

    def accumulate(j, p_ref, al_ref):
        k0 = key_start(j)
        acc_ref[...] = al_ref[...] * acc_ref[...] + jnp.dot(p_ref[...], va_ref[pl.ds(k0, tk), :],
                                                            preferred_element_type=F32)

    sa, sb = (sa_ref, mxa_ref), (sb_ref, mxb_ref)
    pa, pb = (pa_ref, ala_ref), (pb_ref, alb_ref)
    scores(0, *sa)
    if nk > 1:
        probs(0, *sa, *pa)
        scores(1, *sb)

        def pair(i, carry):
            j = 2 * i + 1
            accumulate(j - 1, *pa)
            probs(j, *sb, *pb)
            scores(j + 1, *sa)
            accumulate(j, *pb)
            probs(j + 1, *sa, *pa)
            scores(j + 2, *sb)
            return carry
        lax.fori_loop(0, nk // 2 - 1, pair, 0)
        accumulate(nk - 2, *pa)
        probs(nk - 1, *sb, *pb)
        accumulate(nk - 1, *pb)
    else:
        probs(0, *sa, *pa)
        accumulate(0, *pa)

    lp = lamp_ref[...]
    lam = (jnp.exp(jnp.sum(lp[0:1] * lp[1:2], axis=1, keepdims=True))
           - jnp.exp(jnp.sum(lp[2:3] * lp[3:4], axis=1, keepdims=True)) + sc_ref[ATT_HEADS])
    num = acc_ref[:, 0:HEAD_W]
    den = acc_ref[:, HEAD_W:2 * HEAD_W]
    o = num[0:tq] / den[0:tq] - lam * (num[tq:rows] / den[tq:rows])
    ms = jnp.mean(o * o, axis=-1, keepdims=True)
    o_ref[...] = (o * lax.rsqrt(ms + LN_EPS) * g_ref[...] * sc_ref[ATT_HEADS + 1]).astype(o_ref.dtype)


def _attention(scal, lam_params, qkv, norm_g, *, row_off, batch, seq, n_rows, tq=256, tk=1024):
    tk = min(tk, seq)
    qb0 = row_off // tq
    kb0 = row_off // seq
    nqb = seq // tq
    return pl.pallas_call(
        functools.partial(_attn_kernel, tq=tq, tk=tk, seq=seq),
        grid=(batch, ATT_HEADS, nqb),
        in_specs=[
            pl.BlockSpec(memory_space=pltpu.SMEM),
            pl.BlockSpec((4, ATT_DK), lambda b, h, i: (0, 0)),
            pl.BlockSpec((tq, HEAD_W), lambda b, h, i: (qb0 + b * nqb + i, h)),
            pl.BlockSpec((seq, HEAD_W), lambda b, h, i: (kb0 + b, ATT_HEADS + h)),
            pl.BlockSpec((seq, HEAD_W), lambda b, h, i: (kb0 + b, 2 * ATT_HEADS + h)),
            pl.BlockSpec((1, HEAD_W), lambda b, h, i: (0, 0)),
        ],
        out_specs=pl.BlockSpec((tq, HEAD_W), lambda b, h, i: (b * nqb + i, h)),
        out_shape=jax.ShapeDtypeStruct((n_rows, ATT_HEADS * HEAD_W), BF16),
        scratch_shapes=[pltpu.VMEM((2 * tq, HEAD_W), BF16),
                        pltpu.VMEM((2 + tk // tq, 2 * tq, tk), F32),
                        pltpu.VMEM((seq, 2 * HEAD_W), BF16),
                        pltpu.VMEM((2 * tq, tk), F32), pltpu.VMEM((2 * tq, tk), F32),
                        pltpu.VMEM((2 * tq, tk), BF16), pltpu.VMEM((2 * tq, tk), BF16),
                        pltpu.VMEM((2 * tq, 1), F32), pltpu.VMEM((2 * tq, 1), F32),
                        pltpu.VMEM((2 * tq, 1), F32), pltpu.VMEM((2 * tq, 1), F32),
                        pltpu.VMEM((2 * tq, 1), F32),
                        pltpu.VMEM((2 * tq, 2 * HEAD_W), F32)],
        compiler_params=_params(("parallel", "parallel", "arbitrary")),
        name="diff_attention",
    )(scal, lam_params, qkv, qkv, qkv, norm_g)


def _s5_kernel(u_ref, fm_ref, wb_ref, wcr_ref, wci_ref, ar_ref, ai_ref, y_ref,
               br_ref, bi_ref, xr_ref, xi_ref, *, tc):
    half = S5_STATE // S5_BLOCKS

    @pl.when(pl.program_id(0) == 0)
    def _():
        xr_ref[...] = jnp.zeros(xr_ref.shape, F32)
        xi_ref[...] = jnp.zeros(xi_ref.shape, F32)

    fm = fm_ref[...]
    for j in range(S5_BLOCKS):
        ub = u_ref[:, 128 * j:128 * (j + 1)]
        uf = ub * fm
        bu = jnp.dot(jnp.concatenate([uf, ub - uf], axis=1), wb_ref[j],
                     preferred_element_type=F32)
        br_ref[:, half * j:half * (j + 1)] = bu[:, 0:half]
        bi_ref[:, half * j:half * (j + 1)] = bu[:, half:2 * half]

    def body(t, carry):
        xr, xi = carry
        r0 = pl.multiple_of(t * S5_SLOTS, S5_SLOTS)
        ar = ar_ref[...]
        ai = ai_ref[...]
        nxr = ar * xr - ai * xi + br_ref[pl.ds(r0, S5_SLOTS), :]
        nxi = ar * xi + ai * xr + bi_ref[pl.ds(r0, S5_SLOTS), :]
        br_ref[pl.ds(r0, S5_SLOTS), :] = nxr
        bi_ref[pl.ds(r0, S5_SLOTS), :] = nxi
        return nxr, nxi

    xr, xi = lax.fori_loop(0, tc, body, (xr_ref[...], xi_ref[...]), unroll=4)
    xr_ref[...] = xr
    xi_ref[...] = xi

    is_fwd = fm.astype(F32) > 0.5
    for j in range(S5_BLOCKS):
        sr = br_ref[:, half * j:half * (j + 1)].astype(BF16)
        si = bi_ref[:, half * j:half * (j + 1)].astype(BF16)
        y = (jnp.dot(sr, wcr_ref[j], preferred_element_type=F32)
             + jnp.dot(si, wci_ref[j], preferred_element_type=F32))
        y_ref[:, 128 * j:128 * (j + 1)] = jnp.where(is_fwd, y[:, 0:128], y[:, 128:256])


def _s5_scan(u8, fmask, wb, wcr, wci, ar8, ai8, *, seq, tc=128):
    rows = tc * S5_SLOTS
    return pl.pallas_call(
        functools.partial(_s5_kernel, tc=tc),
        grid=(seq // tc,),
        in_specs=[pl.BlockSpec((rows, S5_W), lambda n: (n, 0)), _const_spec((rows, 128)),
                  _const_spec((S5_BLOCKS, 256, 1024)), _const_spec((S5_BLOCKS, 512, 256)),
                  _const_spec((S5_BLOCKS, 512, 256)),
                  _const_spec((S5_SLOTS, S5_STATE)), _const_spec((S5_SLOTS, S5_STATE))],
        out_specs=pl.BlockSpec((rows, S5_W), lambda n: (n, 0)),
        out_shape=jax.ShapeDtypeStruct((seq * S5_SLOTS, S5_W), F32),
        scratch_shapes=[pltpu.VMEM((rows, S5_STATE), F32), pltpu.VMEM((rows, S5_STATE), F32),
                        pltpu.VMEM((S5_SLOTS, S5_STATE), F32), pltpu.VMEM((S5_SLOTS, S5_STATE), F32)],
        compiler_params=_params(("arbitrary",)),
        name="s5_scan",
    )(u8, fmask, wb, wcr, wci, ar8, ai8)


def _s5_weights(a_re, a_im, log_step, b_re, b_im, c_re, c_im):
    dt = jnp.exp(log_step)[..., None]
    mag = jnp.exp(a_re * dt)
    abr = mag * jnp.cos(a_im * dt)
    abi = mag * jnp.sin(a_im * dt)
    den = a_re * a_re + a_im * a_im
    nr = abr - 1.0
    fr = (nr * a_re + abi * a_im) / den
    fi = (abi * a_re - nr * a_im) / den
    bbr = fr[..., None] * b_re - fi[..., None] * b_im
    bbi = fr[..., None] * b_im + fi[..., None] * b_re
    eye = jnp.eye(8, dtype=F32)

    def in_block(bb):
        bb = bb.reshape(2, S5_BLOCKS, 8, S5_N, S5_GC)
        return jnp.einsum('djgnc,gh->jdgchn', bb, eye).reshape(S5_BLOCKS, 256, 512)

    wb = jnp.concatenate([in_block(bbr), in_block(bbi)], axis=-1).astype(BF16)

    def out_block(cc):
        cc = cc.reshape(2, S5_BLOCKS, 8, S5_GC, S5_N)
        return jnp.einsum('djgcn,gh->jgndhc', cc, eye).reshape(S5_BLOCKS, 512, 256)

    wcr = out_block(c_re).astype(BF16)
    wci = out_block(-c_im).astype(BF16)
    return wb, wcr, wci, abr.reshape(2, S5_STATE), abi.reshape(2, S5_STATE)


def _s5_branch(us5, wb, wcr, wci, abr, abi, *, batch, seq):
    assert 2 * batch <= S5_SLOTS
    u = us5.astype(BF16).reshape(batch, seq, S5_BLOCKS, 128)
    pad = jnp.zeros((S5_SLOTS - 2 * batch, seq, S5_BLOCKS, 128), BF16)
    u8 = jnp.concatenate([u, u[:, ::-1], pad], axis=0)
    u8 = u8.transpose(1, 0, 2, 3).reshape(seq * S5_SLOTS, S5_W)
    tc = 128
    fmask = jnp.broadcast_to((jnp.arange(S5_SLOTS) < batch).astype(BF16)[None, :, None],
                             (tc, S5_SLOTS, 128)).reshape(tc * S5_SLOTS, 128)
    slot_dir = jnp.array([0] * batch + [1] * batch, jnp.int32)
    live = jnp.concatenate([jnp.ones((2 * batch, 1), F32),
                            jnp.zeros((S5_SLOTS - 2 * batch, 1), F32)], axis=0)
    sel = jnp.concatenate([slot_dir, jnp.zeros((S5_SLOTS - 2 * batch,), jnp.int32)])
    ar8 = abr[sel] * live
    ai8 = abi[sel] * live
    y8 = _s5_scan(u8, fmask, wb, wcr, wci, ar8, ai8, seq=seq, tc=tc)
    y8 = y8.reshape(seq, S5_SLOTS, S5_BLOCKS, 128)
    yf = y8[:, 0:batch].transpose(1, 0, 2, 3).reshape(batch * seq, S5_W)
    yb = y8[::-1, batch:2 * batch].transpose(1, 0, 2, 3).reshape(batch * seq, S5_W)
    return yf, yb


def _merge_kernel(x_ref, att_ref, sgu_ref, us5_ref, yf_ref, yb_ref, gate_ref,
                  d_ref, gw_ref, gb_ref, wbr_ref, wo_ref, lg_ref, lb_ref, o_ref):
    y = _gelu(us5_ref[...] * d_ref[...] + yf_ref[...] + yb_ref[...])
    glu = _sigmoid(jnp.dot(y.astype(BF16), gw_ref[...], preferred_element_type=F32) + gb_ref[...])
    s5 = (y * glu).astype(BF16)
    branches = (att_ref[...], sgu_ref[...], s5)
    mix = None
    for i in range(N_BRANCH):
        proj = jnp.dot(branches[i], wbr_ref[i], preferred_element_type=F32)
        term = gate_ref[:, i * D_MODEL:(i + 1) * D_MODEL] * proj
        mix = term if mix is None else mix + term
    mixed = jnp.dot(mix.astype(BF16), wo_ref[...], preferred_element_type=F32)
    o_ref[...] = _layernorm(ALPHA * x_ref[...] + mixed, lg_ref[...], lb_ref[...])


def _merge(x, att, sgu, us5, yf, yb, gates, s5_d, glu_w, glu_b, w_branch, w_o, ln_g, ln_b, *, tm=512):
    n = x.shape[0]
    row = lambda w: pl.BlockSpec((tm, w), lambda i: (i, 0))
    return pl.pallas_call(
        _merge_kernel,
        grid=(n // tm,),
        in_specs=[row(D_MODEL), row(S5_W), row(S5_W), row(S5_W), row(S5_W), row(S5_W),
                  row(N_BRANCH * D_MODEL),
                  _const_spec((1, S5_W)), _const_spec((S5_W, S5_W)), _const_spec((1, S5_W)),
                  _const_spec((N_BRANCH, S5_W, D_MODEL)), _const_spec((D_MODEL, D_MODEL)),
                  _const_spec((1, D_MODEL)), _const_spec((1, D_MODEL))],
        out_specs=row(D_MODEL),
        out_shape=jax.ShapeDtypeStruct((n, D_MODEL), F32),
        compiler_params=_params(("parallel",)),
        name="merge",
    )(x, att, sgu, us5, yf, yb, gates, s5_d, glu_w, glu_b, w_branch, w_o, ln_g, ln_b)


def _top_rows(s, k, with_rank=False):
    row = lax.broadcasted_iota(jnp.int32, (k, s.shape[1]), 0)
    out = jnp.zeros((k, s.shape[1]), F32)
    rank = jnp.full(s.shape, float(k), F32)
    for i in range(k):
        m = jnp.max(s, axis=0, keepdims=True)
        out = jnp.where(row == i, m, out)
        hit = s == m
        if with_rank:
            rank = jnp.where(hit, float(i), rank)
        s = jnp.where(hit, -jnp.inf, s)
    return (out, rank) if with_rank else out


def _pair_candidates(v0, v1):
    r8 = lax.broadcasted_iota(jnp.int32, (8, v0.shape[1]), 0)
    r16 = lax.broadcasted_iota(jnp.int32, v0.shape, 0)
    ninf = -jnp.inf
    lo = v0[0:8]
    return jnp.concatenate([
        v0[0:1] + v1,
        v0[1:2] + v1[0:8],
        jnp.where(r16 >= 2, v0 + v1[0:1], ninf),
        jnp.where(r8 >= 2, lo + v1[1:2], ninf),
        jnp.where((r8 >= 2) & (r8 <= 4), lo + v1[2:3], ninf),
        jnp.where((r8 >= 2) & (r8 <= 3), lo + v1[3:4], ninf),
        jnp.where(r8 == 2, lo + v1[4:5], ninf),
    ], axis=0)


PEER_LANES = 128
PEER_PACK = 16
PEER_RGROUP = 4


def _peer_kernel(x_ref, wq_ref, keys_ref, u_ref, vt_ref, lg_ref, lb_ref, o_ref,
                 xb_ref, cnt_ref, e0_ref, rank_ref, e1_ref, gate_ref, acc_ref, *, tm, rpc):
    c = pl.program_id(1)
    strips = [slice(i * PEER_LANES, (i + 1) * PEER_LANES) for i in range(tm // PEER_LANES)]

    def gen_gate(r0):
        shape = (PEER_PACK, PEER_LANES)
        for si, ls in enumerate(strips):
            sums = [[None] * (PEER_NKEYS // PEER_PACK) for _ in range(PEER_RGROUP)]
            for h in range(PEER_HEADS):
                rk = rank_ref[h, si]
                ex = e1_ref[h, si]
                for r in range(PEER_RGROUP):
                    e1 = c * rpc + r0 + r
                    n = jnp.broadcast_to(cnt_ref[h, si, pl.ds(e1, 1), :], shape).astype(BF16)
                    w = jnp.broadcast_to(e0_ref[h, si, pl.ds(e1, 1), :], shape).astype(BF16)
                    for g in range(PEER_NKEYS // PEER_PACK):
                        rows = slice(g * PEER_PACK, (g + 1) * PEER_PACK)
                        term = jnp.where(rk[rows] < n, ex[rows], jnp.zeros(shape, BF16)) * w
                        sums[r][g] = term if sums[r][g] is None else sums[r][g] + term
            for r in range(PEER_RGROUP):
                for g in range(PEER_NKEYS // PEER_PACK):
                    g0 = (r0 + r) * PEER_NKEYS + g * PEER_PACK
                    gate_ref[g0:g0 + PEER_PACK, ls] = sums[r][g]

    @pl.when(c == 0)
    def _():
        xb = x_ref[...].astype(BF16)
        xb_ref[...] = xb
        q = jnp.dot(xb, wq_ref[...], preferred_element_type=F32).astype(BF16)
        for h in range(PEER_HEADS):
            st = []
            for half in range(2):
                c0 = (2 * h + half) * PEER_DHALF
                st.append(lax.dot_general(keys_ref[h, half], q[:, c0:c0 + PEER_DHALF],
                                          (((1,), (1,)), ((), ())), preferred_element_type=F32))
            for si, ls in enumerate(strips):
                a = st[0][:, ls]
                b = st[1][:, ls]
                v0 = _top_rows(a, PEER_TOPK)
                v1, rank1 = _top_rows(b, PEER_TOPK, with_rank=True)
                cs = _top_rows(_pair_candidates(v0, v1), PEER_TOPK)
                tau = cs[PEER_TOPK - 1:PEER_TOPK]
                z = jnp.sum(jnp.exp(cs - cs[0:1]), axis=0, keepdims=True)
                thr = tau - a
                cnt = jnp.zeros(a.shape, F32)
                for j in range(PEER_TOPK):
                    cnt = jnp.where(v1[j:j + 1] >= thr, float(j + 1), cnt)
                cnt_ref[h, si] = cnt
                e0_ref[h, si] = jnp.exp(a - v0[0:1]) * (0.5 / z)
                rank_ref[h, si] = rank1.astype(BF16)
                e1_ref[h, si] = jnp.exp(b - v1[0:1]).astype(BF16)
        acc_ref[...] = jnp.zeros(acc_ref.shape, F32)

    pe = PEER_RGROUP * PEER_NKEYS
    acts = []
    for r0 in range(0, rpc, PEER_RGROUP):
        p0 = r0 * PEER_NKEYS
        hid = lax.dot_general(u_ref[p0:p0 + pe, :], xb_ref[...], (((1,), (1,)), ((), ())),
                              preferred_element_type=F32)
        gen_gate(r0)
        t = jnp.tanh(hid * (GELU_C1 + GELU_C3 * (hid * hid)))
        acts.append(((gate_ref[p0:p0 + pe, :].astype(F32) * hid) * (1.0 + t)).astype(BF16))
    acc_ref[...] += jnp.dot(vt_ref[...], jnp.concatenate(acts, axis=0),
                            preferred_element_type=F32)

    @pl.when(c == pl.num_programs(1) - 1)
    def _():
        out = acc_ref[...].T
        o_ref[...] = _layernorm(ALPHA * x_ref[...] + out, lg_ref[...], lb_ref[...])


def _peer(x, w_q, keys, u_tab, v_tab_t, ln_g, ln_b, *, tm=512, rpc=16):
    n = x.shape[0]
    ec = rpc * PEER_NKEYS
    assert rpc % PEER_RGROUP == 0 and tm % PEER_LANES == 0
    route = (PEER_HEADS, tm // PEER_LANES, PEER_NKEYS, PEER_LANES)
    return pl.pallas_call(
        functools.partial(_peer_kernel, tm=tm, rpc=rpc),
        grid=(n // tm, PEER_EXPERTS // ec),
        in_specs=[pl.BlockSpec((tm, D_MODEL), lambda i, c: (i, 0)),
                  _const_spec((D_MODEL, PEER_HEADS * PEER_DQ)),
                  _const_spec((PEER_HEADS, 2, PEER_NKEYS, PEER_DHALF)),
                  pl.BlockSpec((ec, D_MODEL), lambda i, c: (c, 0)),
                  pl.BlockSpec((D_MODEL, ec), lambda i, c: (0, c)),
                  _const_spec((1, D_MODEL)), _const_spec((1, D_MODEL))],
        out_specs=pl.BlockSpec((tm, D_MODEL), lambda i, c: (i, 0)),
        out_shape=jax.ShapeDtypeStruct((n, D_MODEL), F32),
        scratch_shapes=[pltpu.VMEM((tm, D_MODEL), BF16),
                        pltpu.VMEM(route, F32), pltpu.VMEM(route, F32),
                        pltpu.VMEM(route, BF16), pltpu.VMEM(route, BF16),
                        pltpu.VMEM((ec, tm), BF16),
                        pltpu.VMEM((D_MODEL, tm), F32)],
        compiler_params=_params(("parallel", "arbitrary")),
        name="peer",
    )(x, w_q, keys, u_tab, v_tab_t, ln_g, ln_b)


def _layer(x, p, s5w, *, batch, seq):
    qkv, sgu, us5, gates = _in_proj(x, p['w_in'], p['b_gate'], p['sgu_ln_g'], p['sgu_ln_b'],
                                    p['sgu_w_s'], p['sgu_b_s'])
    att = _attention(p['att_scal'], p['lam_params'], qkv, p['att_norm_g'],
                     row_off=0, batch=batch, seq=seq, n_rows=batch * seq)
    yf, yb = _s5_branch(us5, *s5w, batch=batch, seq=seq)
    x = _merge(x, att, sgu, us5, yf, yb, gates, p['s5_d'], p['s5_glu_w'], p['s5_glu_b'],
               p['w_branch'], p['w_o'], p['ln1_g'], p['ln1_b'])
    return _peer(x, p['peer_w_q'], p['peer_keys'], p['peer_u'], p['peer_v_t'], p['ln2_g'], p['ln2_b'])


def _prepare(w_in, b_gate, lambda_q1, lambda_k1, lambda_q2, lambda_k2, att_norm_g,
             sgu_ln_g, sgu_ln_b, sgu_w_s, sgu_b_s,
             s5_a_re, s5_a_im, s5_log_step, s5_b_re, s5_b_im, s5_c_re, s5_c_im, s5_d, s5_glu_w, s5_glu_b,
             w_branch, w_o, ln1_g, ln1_b, peer_w_q, peer_keys, peer_u, peer_v, ln2_g, ln2_b):
    depth = w_in.shape[0]
    lam_init = 0.8 - 0.6 * jnp.exp(-0.3 * jnp.arange(depth, dtype=F32))
    slopes = jnp.exp2(-8.0 * jnp.arange(1, ATT_HEADS + 1, dtype=F32) / ATT_HEADS)
    att_scal = jnp.concatenate([jnp.broadcast_to(slopes, (depth, ATT_HEADS)),
                                lam_init[:, None], 1.0 - lam_init[:, None]], axis=1)
    row = lambda a: a.reshape(depth, 1, -1).astype(F32)
    return dict(
        w_in=w_in.astype(BF16), b_gate=row(b_gate),
        att_scal=att_scal,
        lam_params=jnp.stack([lambda_q1, lambda_k1, lambda_q2, lambda_k2], axis=1).astype(F32),
        att_norm_g=row(att_norm_g),
        sgu_ln_g=row(sgu_ln_g), sgu_ln_b=row(sgu_ln_b), sgu_w_s=sgu_w_s.astype(BF16),
        sgu_b_s=jnp.repeat(jnp.swapaxes(sgu_b_s, 1, 2), SGU_W // SGU_GROUPS, axis=2).astype(F32),
        s5_a_re=s5_a_re, s5_a_im=s5_a_im, s5_log_step=s5_log_step, s5_b_re=s5_b_re, s5_b_im=s5_b_im,
        s5_c_re=s5_c_re, s5_c_im=s5_c_im,
        s5_d=row(s5_d), s5_glu_w=s5_glu_w.astype(BF16), s5_glu_b=row(s5_glu_b),
        w_branch=w_branch.astype(BF16), w_o=w_o.astype(BF16), ln1_g=row(ln1_g), ln1_b=row(ln1_b),
        peer_w_q=peer_w_q.astype(BF16), peer_keys=peer_keys.astype(BF16),
        peer_u=peer_u.astype(BF16), peer_v_t=jnp.swapaxes(peer_v.astype(BF16), 1, 2),
        ln2_g=row(ln2_g), ln2_b=row(ln2_b),
    )


def _trunk(xs, weights):
    shapes = [x.shape for x in xs]
    params = _prepare(*weights)

    def step(carry, p):
        s5w = _s5_weights(p['s5_a_re'], p['s5_a_im'], p['s5_log_step'],
                          p['s5_b_re'], p['s5_b_im'], p['s5_c_re'], p['s5_c_im'])
        return tuple(_layer(x, p, s5w, batch=b, seq=s)
                     for x, (b, s, _) in zip(carry, shapes)), None

    flat = tuple(x.reshape(-1, D_MODEL) for x in xs)
    flat, _ = lax.scan(step, flat, params)
    return tuple(x.reshape(shape) for x, shape in zip(flat, shapes))


def kernel(x_prompt, x_sample, w_in, b_gate, lambda_q1, lambda_k1, lambda_q2, lambda_k2, att_norm_g, sgu_ln_g, sgu_ln_b, sgu_w_s, sgu_b_s, s5_a_re, s5_a_im, s5_log_step, s5_b_re, s5_b_im, s5_c_re, s5_c_im, s5_d, s5_glu_w, s5_glu_b, w_branch, w_o, ln1_g, ln1_b, peer_w_q, peer_keys, peer_u, peer_v, ln2_g, ln2_b):
    weights = (w_in, b_gate, lambda_q1, lambda_k1, lambda_q2, lambda_k2, att_norm_g,
               sgu_ln_g, sgu_ln_b, sgu_w_s, sgu_b_s,
               s5_a_re, s5_a_im, s5_log_step, s5_b_re, s5_b_im, s5_c_re, s5_c_im, s5_d, s5_glu_w, s5_glu_b,
               w_branch, w_o, ln1_g, ln1_b, peer_w_q, peer_keys, peer_u, peer_v, ln2_g, ln2_b)
    y_prompt, y_sample = _trunk([x_prompt, x_sample], weights)
    return (y_prompt, y_sample)
```

```python
import functools
import math

import jax
import jax.numpy as jnp
from jax import lax
from jax.experimental import pallas as pl
from jax.experimental.pallas import tpu as pltpu

F32 = jnp.float32
BF16 = jnp.bfloat16

D_MODEL = 1024
DEPTH = 4
ATT_HEADS = 4
ATT_DK = 64
HEAD_W = 2 * ATT_DK
QKV_W = 3 * ATT_HEADS * HEAD_W
SGU_GROUPS = 4
SGU_CHUNK = 128
SGU_W = 512
S5_GC = 16
S5_GROUPS = 32
S5_N = 64
S5_W = 512
S5_STATE = S5_GROUPS * S5_N
S5_SLOTS = 8
S5_BLOCKS = 4
N_BRANCH = 3
Z_OFF = QKV_W
U_OFF = Z_OFF + 2 * SGU_W
G_OFF = U_OFF + S5_W
IN_W = G_OFF + N_BRANCH * D_MODEL
PEER_HEADS = 8
PEER_NKEYS = 128
PEER_EXPERTS = PEER_NKEYS * PEER_NKEYS
PEER_DQ = 256
PEER_DHALF = 128
PEER_TOPK = 16
ALPHA = (2 * DEPTH) ** 0.25
LN_EPS = 1e-5
NEG_BIG = -1e30

VMEM_LIMIT = 56 * 1024 * 1024


GELU_C1 = 0.7978845608028654
GELU_C3 = GELU_C1 * 0.044715


def _gelu(x):
    return 0.5 * x * (1.0 + jnp.tanh(x * (GELU_C1 + GELU_C3 * (x * x))))


def _sigmoid(x):
    return 1.0 / (1.0 + jnp.exp(-x))


def _layernorm(x, g, b):
    mu = jnp.mean(x, axis=-1, keepdims=True)
    xc = x - mu
    var = jnp.mean(xc * xc, axis=-1, keepdims=True)
    return xc * lax.rsqrt(var + LN_EPS) * g + b


def _const_spec(shape):
    nd = len(shape)
    return pl.BlockSpec(shape, lambda *_: (0,) * nd, pipeline_mode=pl.Buffered(1))


def _params(sem, flags=None):
    return pltpu.CompilerParams(dimension_semantics=sem, vmem_limit_bytes=VMEM_LIMIT, flags=flags)


def _inproj_kernel(x_ref, w_ref, bg_ref, lng_ref, lnb_ref, ws_ref, bs_ref,
                   qkv_ref, sgu_ref, us5_ref, gate_ref, *, tm):
    xb = x_ref[...].astype(BF16)

    h = jnp.dot(xb, w_ref[:, 0:QKV_W], preferred_element_type=F32)
    nq = ATT_HEADS * HEAD_W
    qkv_ref[:, 0:nq] = (h[:, 0:nq] * (ATT_DK ** -0.5)).astype(BF16)
    qkv_ref[:, nq:QKV_W] = h[:, nq:QKV_W].astype(BF16)

    z = _gelu(jnp.dot(xb, w_ref[:, Z_OFF:U_OFF], preferred_element_type=F32))
    u = z[:, 0:SGU_W]
    v = _layernorm(z[:, SGU_W:2 * SGU_W], lng_ref[...], lnb_ref[...]).astype(BF16)
    for ci in range(tm // SGU_CHUNK):
        r0 = ci * SGU_CHUNK
        for g in range(SGU_GROUPS):
            c0 = g * 128
            sv = jnp.dot(ws_ref[g], v[r0:r0 + SGU_CHUNK, c0:c0 + 128],
                         preferred_element_type=F32) + bs_ref[:, c0:c0 + 128]
            sgu_ref[r0:r0 + SGU_CHUNK, c0:c0 + 128] = (
                u[r0:r0 + SGU_CHUNK, c0:c0 + 128] * sv).astype(BF16)

    us5_ref[...] = jnp.dot(xb, w_ref[:, U_OFF:G_OFF], preferred_element_type=F32)

    gl = jnp.dot(xb, w_ref[:, G_OFF:IN_W], preferred_element_type=F32)
    gate_ref[...] = _sigmoid(gl + bg_ref[...])


def _in_proj(x, w_in, b_gate, ln_g, ln_b, w_s, b_s_full, *, tm=256):
    n = x.shape[0]
    row = lambda w: pl.BlockSpec((tm, w), lambda i: (i, 0))
    return pl.pallas_call(
        functools.partial(_inproj_kernel, tm=tm),
        grid=(n // tm,),
        in_specs=[row(D_MODEL), _const_spec((D_MODEL, IN_W)), _const_spec((1, N_BRANCH * D_MODEL)),
                  _const_spec((1, SGU_W)), _const_spec((1, SGU_W)),
                  _const_spec((SGU_GROUPS, SGU_CHUNK, SGU_CHUNK)), _const_spec((SGU_CHUNK, SGU_W))],
        out_specs=[row(QKV_W), row(SGU_W), row(S5_W), row(N_BRANCH * D_MODEL)],
        out_shape=[jax.ShapeDtypeStruct((n, QKV_W), BF16), jax.ShapeDtypeStruct((n, SGU_W), BF16),
                   jax.ShapeDtypeStruct((n, S5_W), F32),
                   jax.ShapeDtypeStruct((n, N_BRANCH * D_MODEL), F32)],
        compiler_params=_params(("parallel",)),
        name="in_proj",
    )(x, w_in, b_gate, ln_g, ln_b, w_s, b_s_full)


def _attn_kernel(sc_ref, lamp_ref, q_ref, k_ref, v_ref, g_ref, o_ref,
                 qs_ref, bias_ref, va_ref, sa_ref, sb_ref, pa_ref, pb_ref,
                 mxa_ref, mxb_ref, ala_ref, alb_ref, m_ref, acc_ref, *, tq, tk, seq):
    hd = pl.program_id(1)
    qi = pl.program_id(2)
    slope = sc_ref[hd]
    nk = seq // tk
    nd = tk // tq
    jd = (qi * tq) // tk
    rows = 2 * tq
    assert nk == 1 or nk % 2 == 0

    @pl.when(qi == 0)
    def _():
        ri = lax.broadcasted_iota(jnp.int32, (rows, tk), 0)
        ri = jnp.where(ri >= tq, ri - tq, ri)
        ci = lax.broadcasted_iota(jnp.int32, (rows, tk), 1)
        rel = (ri - ci).astype(F32)
        bias_ref[0] = slope * rel
        bias_ref[1] = -slope * rel
        for v in range(nd):
            bias_ref[2 + v] = slope * jnp.abs(rel + float(v * tq))
        va_ref[:, 0:HEAD_W] = v_ref[...]
        va_ref[:, HEAD_W:2 * HEAD_W] = jnp.ones((seq, HEAD_W), BF16)

    q = q_ref[...]
    lane = lax.broadcasted_iota(jnp.int32, q.shape, 1)
    zero = jnp.zeros_like(q)
    qs_ref[0:tq, :] = jnp.where(lane < ATT_DK, q, zero)
    qs_ref[tq:rows, :] = jnp.where(lane >= ATT_DK, q, zero)

    m_ref[...] = jnp.full(m_ref.shape, NEG_BIG, F32)
    acc_ref[...] = jnp.zeros(acc_ref.shape, F32)

    def tile_const(j):
        off = jnp.abs(qi * tq - j * tk).astype(F32)
        return jnp.where(j == jd, 0.0, -slope * off)

    def tile_type(j):
        return jnp.where(j < jd, 0, jnp.where(j > jd, 1, 2 + qi % nd))

    def key_start(j):
        return j * tk if isinstance(j, int) else pl.multiple_of(j * tk, tk)

    def scores(j, s_ref, mx_ref):
        k0 = key_start(j)
        s = lax.dot_general(qs_ref[...], k_ref[pl.ds(k0, tk), :], (((1,), (1,)), ((), ())),
                            preferred_element_type=F32)
        s_ref[...] = s - bias_ref[tile_type(j)]
        mx_ref[...] = jnp.max(s_ref[...], axis=1, keepdims=True) + tile_const(j)

    def probs(j, s_ref, mx_ref, p_ref, al_ref):
        m_old = m_ref[...]
        m_new = jnp.maximum(m_old, mx_ref[...])
        al_ref[...] = jnp.exp(m_old - m_new)
        p_ref[...] = jnp.exp(s_ref[...] - (m_new - tile_const(j))).astype(BF16)
        m_ref[...] = m_new

    def accumulate(j, p_ref, al_ref):
        k0 = key_start(j)
        acc_ref[...] = al_ref[...] * acc_ref[...] + jnp.dot(p_ref[...], va_ref[pl.ds(k0, tk), :],
                                                            preferred_element_type=F32)

    sa, sb = (sa_ref, mxa_ref), (sb_ref, mxb_ref)
    pa, pb = (pa_ref, ala_ref), (pb_ref, alb_ref)
    scores(0, *sa)
    if nk > 1:
        probs(0, *sa, *pa)
        scores(1, *sb)

        for i in range(nk // 2 - 1):
            j = 2 * i + 1
            accumulate(j - 1, *pa)
            probs(j, *sb, *pb)
            scores(j + 1, *sa)
            accumulate(j, *pb)
            probs(j + 1, *sa, *pa)
            scores(j + 2, *sb)
        accumulate(nk - 2, *pa)
        probs(nk - 1, *sb, *pb)
        accumulate(nk - 1, *pb)
    else:
        probs(0, *sa, *pa)
        accumulate(0, *pa)

    lp = lamp_ref[...]
    lam = (jnp.exp(jnp.sum(lp[0:1] * lp[1:2], axis=1, keepdims=True))
           - jnp.exp(jnp.sum(lp[2:3] * lp[3:4], axis=1, keepdims=True)) + sc_ref[ATT_HEADS])
    num = acc_ref[:, 0:HEAD_W]
    den = acc_ref[:, HEAD_W:2 * HEAD_W]
    o = num[0:tq] / den[0:tq] - lam * (num[tq:rows] / den[tq:rows])
    ms = jnp.mean(o * o, axis=-1, keepdims=True)
    o_ref[...] = (o * lax.rsqrt(ms + LN_EPS) * g_ref[...] * sc_ref[ATT_HEADS + 1]).astype(o_ref.dtype)


def _attention(scal, lam_params, qkv, norm_g, *, row_off, batch, seq, n_rows, tq=256, tk=1024):
    tk = min(tk, seq)
    qb0 = row_off // tq
    kb0 = row_off // seq
    nqb = seq // tq
    return pl.pallas_call(
        functools.partial(_attn_kernel, tq=tq, tk=tk, seq=seq),
        grid=(batch, ATT_HEADS, nqb),
        in_specs=[
            pl.BlockSpec(memory_space=pltpu.SMEM),
            pl.BlockSpec((4, ATT_DK), lambda b, h, i: (0, 0)),
            pl.BlockSpec((tq, HEAD_W), lambda b, h, i: (qb0 + b * nqb + i, h)),
            pl.BlockSpec((seq, HEAD_W), lambda b, h, i: (kb0 + b, ATT_HEADS + h)),
            pl.BlockSpec((seq, HEAD_W), lambda b, h, i: (kb0 + b, 2 * ATT_HEADS + h)),
            pl.BlockSpec((1, HEAD_W), lambda b, h, i: (0, 0)),
        ],
        out_specs=pl.BlockSpec((tq, HEAD_W), lambda b, h, i: (b * nqb + i, h)),
        out_shape=jax.ShapeDtypeStruct((n_rows, ATT_HEADS * HEAD_W), BF16),
        scratch_shapes=[pltpu.VMEM((2 * tq, HEAD_W), BF16),
                        pltpu.VMEM((2 + tk // tq, 2 * tq, tk), F32),
                        pltpu.VMEM((seq, 2 * HEAD_W), BF16),
                        pltpu.VMEM((2 * tq, tk), F32), pltpu.VMEM((2 * tq, tk), F32),
                        pltpu.VMEM((2 * tq, tk), BF16), pltpu.VMEM((2 * tq, tk), BF16),
                        pltpu.VMEM((2 * tq, 1), F32), pltpu.VMEM((2 * tq, 1), F32),
                        pltpu.VMEM((2 * tq, 1), F32), pltpu.VMEM((2 * tq, 1), F32),
                        pltpu.VMEM((2 * tq, 1), F32),
                        pltpu.VMEM((2 * tq, 2 * HEAD_W), F32)],
        compiler_params=_params(("parallel", "parallel", "arbitrary")),
        name="diff_attention",
    )(scal, lam_params, qkv, qkv, qkv, norm_g)


def _s5_kernel(u_ref, fm_ref, wb_ref, wcr_ref, wci_ref, ar_ref, ai_ref, y_ref,
               br_ref, bi_ref, xr_ref, xi_ref, *, tc):
    half = S5_STATE // S5_BLOCKS

    @pl.when(pl.program_id(0) == 0)
    def _():
        xr_ref[...] = jnp.zeros(xr_ref.shape, F32)
        xi_ref[...] = jnp.zeros(xi_ref.shape, F32)

    fm = fm_ref[...]
    for j in range(S5_BLOCKS):
        ub = u_ref[:, 128 * j:128 * (j + 1)]
        uf = ub * fm
        bu = jnp.dot(jnp.concatenate([uf, ub - uf], axis=1), wb_ref[j],
                     preferred_element_type=F32)
        br_ref[:, half * j:half * (j + 1)] = bu[:, 0:half]
        bi_ref[:, half * j:half * (j + 1)] = bu[:, half:2 * half]

    def body(t, carry):
        xr, xi = carry
        r0 = pl.multiple_of(t * S5_SLOTS, S5_SLOTS)
        ar = ar_ref[...]
        ai = ai_ref[...]
        nxr = ar * xr - ai * xi + br_ref[pl.ds(r0, S5_SLOTS), :]
        nxi = ar * xi + ai * xr + bi_ref[pl.ds(r0, S5_SLOTS), :]
        br_ref[pl.ds(r0, S5_SLOTS), :] = nxr
        bi_ref[pl.ds(r0, S5_SLOTS), :] = nxi
        return nxr, nxi

    xr, xi = lax.fori_loop(0, tc, body, (xr_ref[...], xi_ref[...]), unroll=4)
    xr_ref[...] = xr
    xi_ref[...] = xi

    is_fwd = fm.astype(F32) > 0.5
    for j in range(S5_BLOCKS):
        sr = br_ref[:, half * j:half * (j + 1)].astype(BF16)
        si = bi_ref[:, half * j:half * (j + 1)].astype(BF16)
        y = (jnp.dot(sr, wcr_ref[j], preferred_element_type=F32)
             + jnp.dot(si, wci_ref[j], preferred_element_type=F32))
        y_ref[:, 128 * j:128 * (j + 1)] = jnp.where(is_fwd, y[:, 0:128], y[:, 128:256])


def _s5_scan(u8, fmask, wb, wcr, wci, ar8, ai8, *, seq, tc=128):
    rows = tc * S5_SLOTS
    return pl.pallas_call(
        functools.partial(_s5_kernel, tc=tc),
        grid=(seq // tc,),
        in_specs=[pl.BlockSpec((rows, S5_W), lambda n: (n, 0)), _const_spec((rows, 128)),
                  _const_spec((S5_BLOCKS, 256, 1024)), _const_spec((S5_BLOCKS, 512, 256)),
                  _const_spec((S5_BLOCKS, 512, 256)),
                  _const_spec((S5_SLOTS, S5_STATE)), _const_spec((S5_SLOTS, S5_STATE))],
        out_specs=pl.BlockSpec((rows, S5_W), lambda n: (n, 0)),
        out_shape=jax.ShapeDtypeStruct((seq * S5_SLOTS, S5_W), F32),
        scratch_shapes=[pltpu.VMEM((rows, S5_STATE), F32), pltpu.VMEM((rows, S5_STATE), F32),
                        pltpu.VMEM((S5_SLOTS, S5_STATE), F32), pltpu.VMEM((S5_SLOTS, S5_STATE), F32)],
        compiler_params=_params(("arbitrary",)),
        name="s5_scan",
    )(u8, fmask, wb, wcr, wci, ar8, ai8)


def _s5_weights(a_re, a_im, log_step, b_re, b_im, c_re, c_im):
    dt = jnp.exp(log_step)[..., None]
    mag = jnp.exp(a_re * dt)
    abr = mag * jnp.cos(a_im * dt)
    abi = mag * jnp.sin(a_im * dt)
    den = a_re * a_re + a_im * a_im
    nr = abr - 1.0
    fr = (nr * a_re + abi * a_im) / den
    fi = (abi * a_re - nr * a_im) / den
    bbr = fr[..., None] * b_re - fi[..., None] * b_im
    bbi = fr[..., None] * b_im + fi[..., None] * b_re
    eye = jnp.eye(8, dtype=F32)

    def in_block(bb):
        bb = bb.reshape(2, S5_BLOCKS, 8, S5_N, S5_GC)
        return jnp.einsum('djgnc,gh->jdgchn', bb, eye).reshape(S5_BLOCKS, 256, 512)

    wb = jnp.concatenate([in_block(bbr), in_block(bbi)], axis=-1).astype(BF16)

    def out_block(cc):
        cc = cc.reshape(2, S5_BLOCKS, 8, S5_GC, S5_N)
        return jnp.einsum('djgcn,gh->jgndhc', cc, eye).reshape(S5_BLOCKS, 512, 256)

    wcr = out_block(c_re).astype(BF16)
    wci = out_block(-c_im).astype(BF16)
    return wb, wcr, wci, abr.reshape(2, S5_STATE), abi.reshape(2, S5_STATE)


def _s5_branch(us5, wb, wcr, wci, abr, abi, *, batch, seq):
    assert 2 * batch <= S5_SLOTS
    u = us5.astype(BF16).reshape(batch, seq, S5_BLOCKS, 128)
    pad = jnp.zeros((S5_SLOTS - 2 * batch, seq, S5_BLOCKS, 128), BF16)
    u8 = jnp.concatenate([u, u[:, ::-1], pad], axis=0)
    u8 = u8.transpose(1, 0, 2, 3).reshape(seq * S5_SLOTS, S5_W)
    tc = 128
    fmask = jnp.broadcast_to((jnp.arange(S5_SLOTS) < batch).astype(BF16)[None, :, None],
                             (tc, S5_SLOTS, 128)).reshape(tc * S5_SLOTS, 128)
    slot_dir = jnp.array([0] * batch + [1] * batch, jnp.int32)
    live = jnp.concatenate([jnp.ones((2 * batch, 1), F32),
                            jnp.zeros((S5_SLOTS - 2 * batch, 1), F32)], axis=0)
    sel = jnp.concatenate([slot_dir, jnp.zeros((S5_SLOTS - 2 * batch,), jnp.int32)])
    ar8 = abr[sel] * live
    ai8 = abi[sel] * live
    y8 = _s5_scan(u8, fmask, wb, wcr, wci, ar8, ai8, seq=seq, tc=tc)
    y8 = y8.reshape(seq, S5_SLOTS, S5_BLOCKS, 128)
    yf = y8[:, 0:batch].transpose(1, 0, 2, 3).reshape(batch * seq, S5_W)
    yb = y8[::-1, batch:2 * batch].transpose(1, 0, 2, 3).reshape(batch * seq, S5_W)
    return yf, yb


def _merge_kernel(x_ref, att_ref, sgu_ref, us5_ref, yf_ref, yb_ref, gate_ref,
                  d_ref, gw_ref, gb_ref, wbr_ref, wo_ref, lg_ref, lb_ref, o_ref):
    y = _gelu(us5_ref[...] * d_ref[...] + yf_ref[...] + yb_ref[...])
    glu = _sigmoid(jnp.dot(y.astype(BF16), gw_ref[...], preferred_element_type=F32) + gb_ref[...])
    s5 = (y * glu).astype(BF16)
    branches = (att_ref[...], sgu_ref[...], s5)
    mix = None
    for i in range(N_BRANCH):
        proj = jnp.dot(branches[i], wbr_ref[i], preferred_element_type=F32)
        term = gate_ref[:, i * D_MODEL:(i + 1) * D_MODEL] * proj
        mix = term if mix is None else mix + term
    mixed = jnp.dot(mix.astype(BF16), wo_ref[...], preferred_element_type=F32)
    o_ref[...] = _layernorm(ALPHA * x_ref[...] + mixed, lg_ref[...], lb_ref[...])


def _merge(x, att, sgu, us5, yf, yb, gates, s5_d, glu_w, glu_b, w_branch, w_o, ln_g, ln_b, *, tm=512):
    n = x.shape[0]
    row = lambda w: pl.BlockSpec((tm, w), lambda i: (i, 0))
    return pl.pallas_call(
        _merge_kernel,
        grid=(n // tm,),
        in_specs=[row(D_MODEL), row(S5_W), row(S5_W), row(S5_W), row(S5_W), row(S5_W),
                  row(N_BRANCH * D_MODEL),
                  _const_spec((1, S5_W)), _const_spec((S5_W, S5_W)), _const_spec((1, S5_W)),
                  _const_spec((N_BRANCH, S5_W, D_MODEL)), _const_spec((D_MODEL, D_MODEL)),
                  _const_spec((1, D_MODEL)), _const_spec((1, D_MODEL))],
        out_specs=row(D_MODEL),
        out_shape=jax.ShapeDtypeStruct((n, D_MODEL), F32),
        compiler_params=_params(("parallel",)),
        name="merge",
    )(x, att, sgu, us5, yf, yb, gates, s5_d, glu_w, glu_b, w_branch, w_o, ln_g, ln_b)


def _top_rows(s, k, with_rank=False):
    row = lax.broadcasted_iota(jnp.int32, (k, s.shape[1]), 0)
    out = jnp.zeros((k, s.shape[1]), F32)
    rank = jnp.full(s.shape, float(k), F32)
    for i in range(k):
        m = jnp.max(s, axis=0, keepdims=True)
        out = jnp.where(row == i, m, out)
        hit = s == m
        if with_rank:
            rank = jnp.where(hit, float(i), rank)
        s = jnp.where(hit, -jnp.inf, s)
    return (out, rank) if with_rank else out


def _pair_candidates(v0, v1):
    r8 = lax.broadcasted_iota(jnp.int32, (8, v0.shape[1]), 0)
    r16 = lax.broadcasted_iota(jnp.int32, v0.shape, 0)
    ninf = -jnp.inf
    lo = v0[0:8]
    return jnp.concatenate([
        v0[0:1] + v1,
        v0[1:2] + v1[0:8],
        jnp.where(r16 >= 2, v0 + v1[0:1], ninf),
        jnp.where(r8 >= 2, lo + v1[1:2], ninf),
        jnp.where((r8 >= 2) & (r8 <= 4), lo + v1[2:3], ninf),
        jnp.where((r8 >= 2) & (r8 <= 3), lo + v1[3:4], ninf),
        jnp.where(r8 == 2, lo + v1[4:5], ninf),
    ], axis=0)


PEER_LANES = 128
PEER_PACK = 16
PEER_RGROUP = 4


def _peer_kernel(x_ref, wq_ref, keys_ref, u_ref, vt_ref, lg_ref, lb_ref, o_ref,
                 xb_ref, cnt_ref, e0_ref, rank_ref, e1_ref, gate_ref, acc_ref, *, tm, rpc):
    c = pl.program_id(1)
    strips = [slice(i * PEER_LANES, (i + 1) * PEER_LANES) for i in range(tm // PEER_LANES)]

    def gen_gate(r0):
        shape = (PEER_PACK, PEER_LANES)
        for si, ls in enumerate(strips):
            sums = [[None] * (PEER_NKEYS // PEER_PACK) for _ in range(PEER_RGROUP)]
            for h in range(PEER_HEADS):
                rk = rank_ref[h, si]
                ex = e1_ref[h, si]
                for r in range(PEER_RGROUP):
                    e1 = c * rpc + r0 + r
                    n = jnp.broadcast_to(cnt_ref[h, si, pl.ds(e1, 1), :], shape).astype(BF16)
                    w = jnp.broadcast_to(e0_ref[h, si, pl.ds(e1, 1), :], shape).astype(BF16)
                    for g in range(PEER_NKEYS // PEER_PACK):
                        rows = slice(g * PEER_PACK, (g + 1) * PEER_PACK)
                        term = jnp.where(rk[rows] < n, ex[rows], jnp.zeros(shape, BF16)) * w
                        sums[r][g] = term if sums[r][g] is None else sums[r][g] + term
            for r in range(PEER_RGROUP):
                for g in range(PEER_NKEYS // PEER_PACK):
                    g0 = (r0 + r) * PEER_NKEYS + g * PEER_PACK
                    gate_ref[g0:g0 + PEER_PACK, ls] = sums[r][g]

    @pl.when(c == 0)
    def _():
        xb = x_ref[...].astype(BF16)
        xb_ref[...] = xb
        q = jnp.dot(xb, wq_ref[...], preferred_element_type=F32).astype(BF16)
        for h in range(PEER_HEADS):
            st = []
            for half in range(2):
                c0 = (2 * h + half) * PEER_DHALF
                st.append(lax.dot_general(keys_ref[h, half], q[:, c0:c0 + PEER_DHALF],
                                          (((1,), (1,)), ((), ())), preferred_element_type=F32))
            for si, ls in enumerate(strips):
                a = st[0][:, ls]
                b = st[1][:, ls]
                v0 = _top_rows(a, PEER_TOPK)
                v1, rank1 = _top_rows(b, PEER_TOPK, with_rank=True)
                cs = _top_rows(_pair_candidates(v0, v1), PEER_TOPK)
                tau = cs[PEER_TOPK - 1:PEER_TOPK]
                z = jnp.sum(jnp.exp(cs - cs[0:1]), axis=0, keepdims=True)
                thr = tau - a
                cnt = jnp.zeros(a.shape, F32)
                for j in range(PEER_TOPK):
                    cnt = jnp.where(v1[j:j + 1] >= thr, float(j + 1), cnt)
                cnt_ref[h, si] = cnt
                e0_ref[h, si] = jnp.exp(a - v0[0:1]) * (0.5 / z)
                rank_ref[h, si] = rank1.astype(BF16)
                e1_ref[h, si] = jnp.exp(b - v1[0:1]).astype(BF16)
        acc_ref[...] = jnp.zeros(acc_ref.shape, F32)

    pe = PEER_RGROUP * PEER_NKEYS
    acts = []
    for r0 in range(0, rpc, PEER_RGROUP):
        p0 = r0 * PEER_NKEYS
        hid = lax.dot_general(u_ref[p0:p0 + pe, :], xb_ref[...], (((1,), (1,)), ((), ())),
                              preferred_element_type=F32)
        gen_gate(r0)
        t = jnp.tanh(hid * (GELU_C1 + GELU_C3 * (hid * hid)))
        acts.append(((gate_ref[p0:p0 + pe, :].astype(F32) * hid) * (1.0 + t)).astype(BF16))
    acc_ref[...] += jnp.dot(vt_ref[...], jnp.concatenate(acts, axis=0),
                            preferred_element_type=F32)

    @pl.when(c == pl.num_programs(1) - 1)
    def _():
        out = acc_ref[...].T
        o_ref[...] = _layernorm(ALPHA * x_ref[...] + out, lg_ref[...], lb_ref[...])


def _peer(x, w_q, keys, u_tab, v_tab_t, ln_g, ln_b, *, tm=512, rpc=16):
    n = x.shape[0]
    ec = rpc * PEER_NKEYS
    assert rpc % PEER_RGROUP == 0 and tm % PEER_LANES == 0
    route = (PEER_HEADS, tm // PEER_LANES, PEER_NKEYS, PEER_LANES)
    return pl.pallas_call(
        functools.partial(_peer_kernel, tm=tm, rpc=rpc),
        grid=(n // tm, PEER_EXPERTS // ec),
        in_specs=[pl.BlockSpec((tm, D_MODEL), lambda i, c: (i, 0)),
                  _const_spec((D_MODEL, PEER_HEADS * PEER_DQ)),
                  _const_spec((PEER_HEADS, 2, PEER_NKEYS, PEER_DHALF)),
                  pl.BlockSpec((ec, D_MODEL), lambda i, c: (c, 0)),
                  pl.BlockSpec((D_MODEL, ec), lambda i, c: (0, c)),
                  _const_spec((1, D_MODEL)), _const_spec((1, D_MODEL))],
        out_specs=pl.BlockSpec((tm, D_MODEL), lambda i, c: (i, 0)),
        out_shape=jax.ShapeDtypeStruct((n, D_MODEL), F32),
        scratch_shapes=[pltpu.VMEM((tm, D_MODEL), BF16),
                        pltpu.VMEM(route, F32), pltpu.VMEM(route, F32),
                        pltpu.VMEM(route, BF16), pltpu.VMEM(route, BF16),
                        pltpu.VMEM((ec, tm), BF16),
                        pltpu.VMEM((D_MODEL, tm), F32)],
        compiler_params=_params(("parallel", "arbitrary")),
        name="peer",
    )(x, w_q, keys, u_tab, v_tab_t, ln_g, ln_b)


def _layer(x, p, s5w, *, batch, seq):
    qkv, sgu, us5, gates = _in_proj(x, p['w_in'], p['b_gate'], p['sgu_ln_g'], p['sgu_ln_b'],
                                    p['sgu_w_s'], p['sgu_b_s'])
    att = _attention(p['att_scal'], p['lam_params'], qkv, p['att_norm_g'],
                     row_off=0, batch=batch, seq=seq, n_rows=batch * seq)
    yf, yb = _s5_branch(us5, *s5w, batch=batch, seq=seq)
    x = _merge(x, att, sgu, us5, yf, yb, gates, p['s5_d'], p['s5_glu_w'], p['s5_glu_b'],
               p['w_branch'], p['w_o'], p['ln1_g'], p['ln1_b'])
    return _peer(x, p['peer_w_q'], p['peer_keys'], p['peer_u'], p['peer_v_t'], p['ln2_g'], p['ln2_b'])


def _prepare(w_in, b_gate, lambda_q1, lambda_k1, lambda_q2, lambda_k2, att_norm_g,
             sgu_ln_g, sgu_ln_b, sgu_w_s, sgu_b_s,
             s5_a_re, s5_a_im, s5_log_step, s5_b_re, s5_b_im, s5_c_re, s5_c_im, s5_d, s5_glu_w, s5_glu_b,
             w_branch, w_o, ln1_g, ln1_b, peer_w_q, peer_keys, peer_u, peer_v, ln2_g, ln2_b):
    depth = w_in.shape[0]
    lam_init = 0.8 - 0.6 * jnp.exp(-0.3 * jnp.arange(depth, dtype=F32))
    slopes = jnp.exp2(-8.0 * jnp.arange(1, ATT_HEADS + 1, dtype=F32) / ATT_HEADS)
    att_scal = jnp.concatenate([jnp.broadcast_to(slopes, (depth, ATT_HEADS)),
                                lam_init[:, None], 1.0 - lam_init[:, None]], axis=1)
    row = lambda a: a.reshape(depth, 1, -1).astype(F32)
    return dict(
        w_in=w_in.astype(BF16), b_gate=row(b_gate),
        att_scal=att_scal,
        lam_params=jnp.stack([lambda_q1, lambda_k1, lambda_q2, lambda_k2], axis=1).astype(F32),
        att_norm_g=row(att_norm_g),
        sgu_ln_g=row(sgu_ln_g), sgu_ln_b=row(sgu_ln_b), sgu_w_s=sgu_w_s.astype(BF16),
        sgu_b_s=jnp.repeat(jnp.swapaxes(sgu_b_s, 1, 2), SGU_W // SGU_GROUPS, axis=2).astype(F32),
        s5_a_re=s5_a_re, s5_a_im=s5_a_im, s5_log_step=s5_log_step, s5_b_re=s5_b_re, s5_b_im=s5_b_im,
        s5_c_re=s5_c_re, s5_c_im=s5_c_im,
        s5_d=row(s5_d), s5_glu_w=s5_glu_w.astype(BF16), s5_glu_b=row(s5_glu_b),
        w_branch=w_branch.astype(BF16), w_o=w_o.astype(BF16), ln1_g=row(ln1_g), ln1_b=row(ln1_b),
        peer_w_q=peer_w_q.astype(BF16), peer_keys=peer_keys.astype(BF16),
        peer_u=peer_u.astype(BF16), peer_v_t=jnp.swapaxes(peer_v.astype(BF16), 1, 2),
        ln2_g=row(ln2_g), ln2_b=row(ln2_b),
    )


def _trunk(xs, weights):
    shapes = [x.shape for x in xs]
    params = _prepare(*weights)

    def step(carry, p):
        s5w = _s5_weights(p['s5_a_re'], p['s5_a_im'], p['s5_log_step'],
                          p['s5_b_re'], p['s5_b_im'], p['s5_c_re'], p['s5_c_im'])
        return tuple(_layer(x, p, s5w, batch=b, seq=s)
                     for x, (b, s, _) in zip(carry, shapes)), None

    flat = tuple(x.reshape(-1, D_MODEL) for x in xs)
    flat, _ = lax.scan(step, flat, params)
    return tuple(x.reshape(shape) for x, shape in zip(flat, shapes))


def kernel(x_prompt, x_sample, w_in, b_gate, lambda_q1, lambda_k1, lambda_q2, lambda_k2, att_norm_g, sgu_ln_g, sgu_ln_b, sgu_w_s, sgu_b_s, s5_a_re, s5_a_im, s5_log_step, s5_b_re, s5_b_im, s5_c_re, s5_c_im, s5_d, s5_glu_w, s5_glu_b, w_branch, w_o, ln1_g, ln1_b, peer_w_q, peer_keys, peer_u, peer_v, ln2_g, ln2_b):
    weights = (w_in, b_gate, lambda_q1, lambda_k1, lambda_q2, lambda_k2, att_norm_g,
               sgu_ln_g, sgu_ln_b, sgu_w_s, sgu_b_s,
               s5_a_re, s5_a_im, s5_log_step, s5_b_re, s5_b_im, s5_c_re, s5_c_im, s5_d, s5_glu_w, s5_glu_b,
               w_branch, w_o, ln1_g, ln1_b, peer_w_q, peer_keys, peer_u, peer_v, ln2_g, ln2_b)
    y_prompt, y_sample = _trunk([x_prompt, x_sample], weights)
    return (y_prompt, y_sample)
```
